```python
import functools
import jax
import jax.numpy as jnp
from jax import lax
import numpy as np

D_MODEL = 1024
BATCH = 8
SEQ = 2048
DEPTH = 1
DEC_BATCH = 32
DEC_SEQ = 1
PAST_LEN = 16384
PAGE_SIZE = 128

D_RNN = D_MODEL
RNN_BLOCKS = 8
RNN_BW = D_RNN // RNN_BLOCKS
CONV_W = 4
LRU_C = 8.0
N_HEADS = 8
HEAD_DIM = D_MODEL // N_HEADS
KV_HEADS = N_HEADS
D_ATTN = N_HEADS * HEAD_DIM
KV_DIM = KV_HEADS * HEAD_DIM
MOBA_BLOCK = 256
MOBA_TOPK = 3
Q_CHUNK = 64
D_FF = 4 * D_MODEL
SPLIT_SIZES = (D_RNN, D_RNN, D_ATTN, KV_DIM, KV_DIM, D_MODEL, D_MODEL)
D_IN = sum(SPLIT_SIZES)
SPLIT_POINTS = tuple(int(s) for s in np.cumsum(SPLIT_SIZES)[:-1])
ALPHA = (2.0 * DEPTH) ** 0.25
BETA = (8.0 * DEPTH) ** -0.25
LN_EPS = 1e-5
NEG = -1e30

kernel_name = 'griffin_moba_hybrid_step'


def layer_norm(x, g, b):
    xf = x.astype(jnp.float32)
    mu = jnp.mean(xf, axis=-1, keepdims=True)
    var = jnp.mean(jnp.square(xf - mu), axis=-1, keepdims=True)
    return ((xf - mu) * lax.rsqrt(var + LN_EPS) * g.astype(jnp.float32) + b.astype(jnp.float32)).astype(x.dtype)


def causal_conv(u, prefix, conv_w, conv_b):
    t = u.shape[1]
    full = jnp.concatenate([prefix.astype(u.dtype), u], axis=1)
    out = conv_b + sum(full[:, j:j + t] * conv_w[j] for j in range(CONV_W))
    return out, full[:, t:]


def rg_lru(xc, h0, w_rg_a, b_rg_a, w_rg_x, b_rg_x, lru_lambda):
    b, t, _ = xc.shape
    xb = xc.reshape(b, t, RNN_BLOCKS, RNN_BW)
    r = jax.nn.sigmoid(jnp.einsum('btnc,ncd->btnd', xb, w_rg_a).reshape(b, t, D_RNN) + b_rg_a)
    i = jax.nn.sigmoid(jnp.einsum('btnc,ncd->btnd', xb, w_rg_x).reshape(b, t, D_RNN) + b_rg_x)
    log_a = -LRU_C * r.astype(jnp.float32) * jax.nn.softplus(-lru_lambda.astype(jnp.float32))
    a = jnp.exp(log_a)
    u_in = jnp.sqrt(-jnp.expm1(2.0 * log_a)) * (i * xc).astype(jnp.float32)

    def step(h, au):
        a_t, u_t = au
        h = a_t * h + u_t
        return h, h

    h_last, hs = lax.scan(step, h0.astype(jnp.float32), (jnp.swapaxes(a, 0, 1), jnp.swapaxes(u_in, 0, 1)))
    return jnp.swapaxes(hs, 0, 1).astype(xc.dtype), h_last.astype(h0.dtype)


def moba_prompt(q, k, v):
    s_len, n_h = q.shape[1], q.shape[2]
    nb = -(-s_len // MOBA_BLOCK)
    pad = nb * MOBA_BLOCK - s_len
    n_sel = min(MOBA_TOPK, nb - 1)
    n_chunks = s_len // Q_CHUNK
    scale = HEAD_DIM ** -0.5

    def per_seq(args):
        q_s, k_s, v_s = args
        k_blk = jnp.pad(k_s, ((0, pad), (0, 0), (0, 0))).reshape(nb, MOBA_BLOCK, n_h, HEAD_DIM).transpose(2, 0, 1, 3)
        v_blk = jnp.pad(v_s, ((0, pad), (0, 0), (0, 0))).reshape(nb, MOBA_BLOCK, n_h, HEAD_DIM).transpose(2, 0, 1, 3)
        k_mean = jnp.mean(k_blk.astype(jnp.float32), axis=2)

        def per_chunk(c):
            start = c * Q_CHUNK
            q_c = lax.dynamic_slice_in_dim(q_s, start, Q_CHUNK, 0).transpose(1, 0, 2)
            q_pos = start + jnp.arange(Q_CHUNK)
            ob = start // MOBA_BLOCK
            k_own = lax.dynamic_index_in_dim(k_blk, ob, axis=1, keepdims=False)
            v_own = lax.dynamic_index_in_dim(v_blk, ob, axis=1, keepdims=False)
            k_pos = ob * MOBA_BLOCK + jnp.arange(MOBA_BLOCK)
            s_own = jnp.einsum('hqd,hkd->hqk', q_c, k_own).astype(jnp.float32) * scale
            s_own = jnp.where(k_pos[None, None, :] <= q_pos[None, :, None], s_own, NEG)
            if n_sel == 0:
                p = jax.nn.softmax(s_own, axis=-1).astype(v_s.dtype)
                out = jnp.einsum('hqk,hkd->hqd', p, v_own)
            else:
                gate = jnp.einsum('hqd,hnd->hqn', q_c.astype(jnp.float32), k_mean)
                gate = jnp.where(jnp.arange(nb)[None, None, :] < ob, gate, NEG)
                g_val, g_idx = lax.top_k(gate, n_sel)
                valid = g_val > NEG / 2
                h_idx = jnp.arange(n_h)[:, None, None]
                k_sel = k_blk[h_idx, g_idx]
                v_sel = v_blk[h_idx, g_idx]
                s_sel = jnp.einsum('hqd,hqnkd->hqnk', q_c, k_sel).astype(jnp.float32) * scale
                s_sel = jnp.where(valid[..., None], s_sel, NEG).reshape(n_h, Q_CHUNK, n_sel * MOBA_BLOCK)
                p = jax.nn.softmax(jnp.concatenate([s_sel, s_own], axis=-1), axis=-1).astype(v_s.dtype)
                p_sel = p[..., :n_sel * MOBA_BLOCK].reshape(n_h, Q_CHUNK, n_sel, MOBA_BLOCK)
                p_own = p[..., n_sel * MOBA_BLOCK:]
                out = (jnp.einsum('hqnk,hqnkd->hqd', p_sel, v_sel)
                       + jnp.einsum('hqk,hkd->hqd', p_own, v_own))
            return out.transpose(1, 0, 2)

        o = lax.map(per_chunk, jnp.arange(n_chunks))
        return o.reshape(s_len, n_h, HEAD_DIM)

    return lax.map(per_seq, (q, k, v))


def moba_sample(q, k_new, v_new, *, cache_k, cache_v, k_page_mean, page_table, layer):
    db, t = q.shape[0], q.shape[1]
    past_len = page_table.shape[1] * PAGE_SIZE
    ppb = MOBA_BLOCK // PAGE_SIZE
    n_full = past_len // MOBA_BLOCK
    rem_pages = (past_len - n_full * MOBA_BLOCK) // PAGE_SIZE
    n_sel = min(MOBA_TOPK, n_full)
    scale = HEAD_DIM ** -0.5
    causal = jnp.arange(t)[None, :] <= jnp.arange(t)[:, None]
    s_new = jnp.einsum('bqhd,bkhd->bhqk', q, k_new).astype(jnp.float32) * scale
    s_parts = [jnp.where(causal, s_new, NEG)]
    v_parts = [('bhqk,bkhd->bqhd', v_new)]
    if rem_pages > 0:
        pref = page_table[:, n_full * ppb:]
        k_pref = cache_k[layer, pref].reshape(db, rem_pages * PAGE_SIZE, KV_HEADS, HEAD_DIM)
        v_pref = cache_v[layer, pref].reshape(db, rem_pages * PAGE_SIZE, KV_HEADS, HEAD_DIM)
        s_parts.append(jnp.einsum('bqhd,bkhd->bhqk', q, k_pref).astype(jnp.float32) * scale)
        v_parts.append(('bhqk,bkhd->bqhd', v_pref))
    if n_sel > 0:
        full_pages = page_table[:, :n_full * ppb]
        k_mean = jnp.mean(k_page_mean[full_pages].reshape(db, n_full, ppb, KV_HEADS, HEAD_DIM), axis=2)
        gate = jnp.einsum('bqhd,bnhd->bhqn', q.astype(jnp.float32), k_mean)
        _, sel = lax.top_k(gate, n_sel)
        logical = sel[..., None] * ppb + jnp.arange(ppb)
        b_idx = jnp.arange(db)[:, None, None, None, None]
        phys = page_table[b_idx, logical][..., None]
        rows = jnp.arange(PAGE_SIZE)
        h_idx = jnp.arange(KV_HEADS)[None, :, None, None, None, None]
        k_sel = cache_k[layer, phys, rows, h_idx].reshape(db, KV_HEADS, t, n_sel * MOBA_BLOCK, HEAD_DIM)
        v_sel = cache_v[layer, phys, rows, h_idx].reshape(db, KV_HEADS, t, n_sel * MOBA_BLOCK, HEAD_DIM)
        s_parts.append(jnp.einsum('bqhd,bhqkd->bhqk', q, k_sel).astype(jnp.float32) * scale)
        v_parts.append(('bhqk,bhqkd->bqhd', v_sel))
    p = jax.nn.softmax(jnp.concatenate(s_parts, axis=-1), axis=-1).astype(v_new.dtype)
    out = 0.0
    off = 0
    for s_i, (spec, v_i) in zip(s_parts, v_parts):
        w = s_i.shape[-1]
        out = out + jnp.einsum(spec, p[..., off:off + w], v_i)
        off += w
    return out


def decoder_layer(x, conv_prefix, h0, attn_fn, w_in, conv_w, conv_b, w_rg_a, b_rg_a, w_rg_x, b_rg_x,
                  lru_lambda, w_br_rnn, w_br_attn, w_out, ln1_g, ln1_b, w_up, b_up, w_down, b_down,
                  ln2_g, ln2_b):
    b, t, _ = x.shape
    u, g, q, k, v, gate_rnn, gate_attn = jnp.split(x @ w_in, SPLIT_POINTS, axis=-1)
    uc, conv_state = causal_conv(u, conv_prefix, conv_w, conv_b)
    hs, h_last = rg_lru(uc, h0, w_rg_a, b_rg_a, w_rg_x, b_rg_x, lru_lambda)
    y_rnn = (hs * jax.nn.gelu(g)) @ w_br_rnn
    k = k.reshape(b, t, KV_HEADS, HEAD_DIM)
    v = v.reshape(b, t, KV_HEADS, HEAD_DIM)
    o = attn_fn(q.reshape(b, t, N_HEADS, HEAD_DIM), k, v).reshape(b, t, D_ATTN)
    y_attn = o @ w_br_attn
    mix = jax.nn.sigmoid(gate_rnn) * y_rnn + jax.nn.sigmoid(gate_attn) * y_attn
    x1 = layer_norm(ALPHA * x + mix @ w_out, ln1_g, ln1_b)
    hid = jnp.square(jax.nn.relu(x1 @ w_up + b_up))
    y = layer_norm(ALPHA * x1 + hid @ w_down + b_down, ln2_g, ln2_b)
    return y, k, v, h_last, conv_state


def setup_inputs(seed: int = 0) -> dict:
    key = jax.random.key(seed)
    ks = jax.random.split(key, 32)
    f32 = jnp.float32
    n_pages = PAST_LEN // PAGE_SIZE
    n_used = DEC_BATCH * n_pages
    n_pool = n_used + max(1, n_used // 4)

    def nrm(k, shape, scale):
        return jax.random.normal(k, shape, f32) * scale

    x_prompt = nrm(ks[0], (BATCH, SEQ, D_MODEL), 1.0)
    x_sample = nrm(ks[1], (DEC_BATCH, DEC_SEQ, D_MODEL), 1.0)
    cache_k = nrm(ks[2], (DEPTH, n_pool, PAGE_SIZE, KV_HEADS, HEAD_DIM), 1.0)
    cache_v = nrm(ks[3], (DEPTH, n_pool, PAGE_SIZE, KV_HEADS, HEAD_DIM), 1.0)
    state_h = nrm(ks[4], (DEPTH, DEC_BATCH, D_RNN), 0.5)
    state_conv = nrm(ks[5], (DEPTH, DEC_BATCH, CONV_W - 1, D_RNN), 1.0)
    page_table = jax.random.permutation(ks[6], n_pool)[:n_used].reshape(DEC_BATCH, n_pages).astype(jnp.int32)
    col_scale = jnp.concatenate([jnp.ones((2 * D_RNN + D_ATTN + KV_DIM,), f32),
                                 jnp.full((KV_DIM,), BETA, f32),
                                 jnp.ones((2 * D_MODEL,), f32)])
    w_in = nrm(ks[7], (DEPTH, D_MODEL, D_IN), D_MODEL ** -0.5) * col_scale
    conv_w = nrm(ks[8], (DEPTH, CONV_W, D_RNN), CONV_W ** -0.5)
    conv_b = nrm(ks[9], (DEPTH, D_RNN), 0.01)
    w_rg_a = nrm(ks[10], (DEPTH, RNN_BLOCKS, RNN_BW, RNN_BW), RNN_BW ** -0.5)
    b_rg_a = nrm(ks[11], (DEPTH, D_RNN), 0.01)
    w_rg_x = nrm(ks[12], (DEPTH, RNN_BLOCKS, RNN_BW, RNN_BW), RNN_BW ** -0.5)
    b_rg_x = nrm(ks[13], (DEPTH, D_RNN), 0.01)
    a0 = jax.random.uniform(ks[14], (DEPTH, D_RNN), f32, 0.9, 0.999)
    s_a = a0 ** (1.0 / LRU_C)
    lru_lambda = jnp.log(s_a) - jnp.log1p(-s_a)
    w_br_rnn = nrm(ks[15], (DEPTH, D_RNN, D_MODEL), D_RNN ** -0.5 * BETA)
    w_br_attn = nrm(ks[16], (DEPTH, D_ATTN, D_MODEL), D_ATTN ** -0.5 * BETA)
    w_out = nrm(ks[17], (DEPTH, D_MODEL, D_MODEL), D_MODEL ** -0.5 * BETA)
    ln1_g = 1.0 + nrm(ks[18], (DEPTH, D_MODEL), 0.02)
    ln1_b = nrm(ks[19], (DEPTH, D_MODEL), 0.02)
    w_up = nrm(ks[20], (DEPTH, D_MODEL, D_FF), D_MODEL ** -0.5 * BETA)
    b_up = nrm(ks[21], (DEPTH, D_FF), 0.01)
    w_down = nrm(ks[22], (DEPTH, D_FF, D_MODEL), D_FF ** -0.5 * BETA)
    b_down = nrm(ks[23], (DEPTH, D_MODEL), 0.01)
    ln2_g = 1.0 + nrm(ks[24], (DEPTH, D_MODEL), 0.02)
    ln2_b = nrm(ks[25], (DEPTH, D_MODEL), 0.02)
    return {'x_prompt': x_prompt, 'x_sample': x_sample, 'cache_k': cache_k, 'cache_v': cache_v,
            'state_h': state_h, 'state_conv': state_conv, 'page_table': page_table,
            'w_in': w_in, 'conv_w': conv_w, 'conv_b': conv_b, 'w_rg_a': w_rg_a, 'b_rg_a': b_rg_a,
            'w_rg_x': w_rg_x, 'b_rg_x': b_rg_x, 'lru_lambda': lru_lambda, 'w_br_rnn': w_br_rnn,
            'w_br_attn': w_br_attn, 'w_out': w_out, 'ln1_g': ln1_g, 'ln1_b': ln1_b, 'w_up': w_up,
            'b_up': b_up, 'w_down': w_down, 'b_down': b_down, 'ln2_g': ln2_g, 'ln2_b': ln2_b}


def reference(x_prompt, x_sample, cache_k, cache_v, state_h, state_conv, page_table, w_in, conv_w, conv_b,
              w_rg_a, b_rg_a, w_rg_x, b_rg_x, lru_lambda, w_br_rnn, w_br_attn, w_out, ln1_g, ln1_b,
              w_up, b_up, w_down, b_down, ln2_g, ln2_b):
    k_page_mean = jnp.mean(cache_k, axis=2, dtype=jnp.float32)
    yp, ys = x_prompt, x_sample
    kp, vp, hp, cp, ks_, vs_, hs_, cs_ = [], [], [], [], [], [], [], []
    for l in range(DEPTH):
        w = (w_in[l], conv_w[l], conv_b[l], w_rg_a[l], b_rg_a[l], w_rg_x[l], b_rg_x[l], lru_lambda[l],
             w_br_rnn[l], w_br_attn[l], w_out[l], ln1_g[l], ln1_b[l], w_up[l], b_up[l], w_down[l],
             b_down[l], ln2_g[l], ln2_b[l])
        zeros_c = jnp.zeros((yp.shape[0], CONV_W - 1, D_RNN), yp.dtype)
        zeros_h = jnp.zeros((yp.shape[0], D_RNN), state_h.dtype)
        yp, k1, v1, h1, c1 = decoder_layer(yp, zeros_c, zeros_h, moba_prompt, *w)
        attn_s = functools.partial(moba_sample, cache_k=cache_k, cache_v=cache_v,
                                   k_page_mean=k_page_mean[l], page_table=page_table, layer=l)
        ys, k2, v2, h2, c2 = decoder_layer(ys, state_conv[l], state_h[l], attn_s, *w)
        kp.append(k1)
        vp.append(v1)
        hp.append(h1)
        cp.append(c1)
        ks_.append(k2)
        vs_.append(v2)
        hs_.append(h2)
        cs_.append(c2)
    return (yp, ys, jnp.stack(kp), jnp.stack(vp), jnp.stack(hp), jnp.stack(cp),
            jnp.stack(ks_), jnp.stack(vs_), jnp.stack(hs_), jnp.stack(cs_))
```

```python
import functools
import math

import jax
import jax.numpy as jnp
from jax import lax
from jax.experimental import pallas as pl
from jax.experimental.pallas import tpu as pltpu

F32 = jnp.float32
BF16 = jnp.bfloat16

D_MODEL = 1024
N_HEADS = 8
HEAD_DIM = 128
RNN_BLOCKS = 8
RNN_BW = 128
CONV_W = 4
LRU_C = 8.0
MOBA_BLOCK = 256
MOBA_TOPK = 3
PAGE_SIZE = 128
PAGES_PER_BLOCK = MOBA_BLOCK // PAGE_SIZE
D_FF = 4096
N_SPLITS = 7
ALPHA = 2.0 ** 0.25
LN_EPS = 1e-5
NEG = -1e30
SCALE = HEAD_DIM ** -0.5

LANES = 128
SUBLANES = 8
SCAN_PAD = 128
VMEM_LIMIT = 56 * 1024 * 1024
PAGE_BUFS = 4

NT_DIMS = (((1,), (1,)), ((), ()))


def _sigmoid(x):
    return 0.5 * jnp.tanh(0.5 * x) + 0.5


def _softplus(x):
    return jnp.maximum(x, 0.0) + jnp.log1p(jnp.exp(-jnp.abs(x)))


def _gelu_tanh(x):
    c = math.sqrt(2.0 / math.pi)
    return x * (0.5 * (1.0 + jnp.tanh(c * (x + 0.044715 * (x * x * x)))))


def _layer_norm(x, g, b):
    mu = jnp.mean(x, axis=-1, keepdims=True)
    xc = x - mu
    var = jnp.mean(xc * xc, axis=-1, keepdims=True)
    return xc * lax.rsqrt(var + LN_EPS) * g + b


def _split_bf16(x):
    hi = x.astype(BF16)
    lo = (x - hi.astype(F32)).astype(BF16)
    return hi, lo


def _rglru_block(uc_blk, ucb_blk, wg, ba, bx, neg_c_sp):
    gz = jnp.dot(ucb_blk, wg, preferred_element_type=F32)
    r = _sigmoid(gz[:, :RNN_BW] + ba)
    ig = _sigmoid(gz[:, RNN_BW:] + bx)
    log_a = (LRU_C * r) * neg_c_sp
    a = jnp.exp(log_a)
    t = jnp.tanh(log_a)
    mult = jnp.sqrt(-2.0 * t / (1.0 - t))
    return a, mult * (ig * uc_blk)


def _front_kernel(x_ref, w_ref, cw_ref, cb_ref, wg_ref, ba_ref, bx_ref, lam_ref,
                  rnn_ref, q_ref, k_ref, v_ref, kb_ref, vb_ref, sga_ref, sgb_ref, sel_ref,
                  h_ref, cs_ref,
                  ubuf, abuf, sbuf, kmt):
    i = pl.program_id(1)
    tt = x_ref.shape[0]

    @pl.when(i == 0)
    def _():
        ubuf[0:SUBLANES, :] = jnp.zeros((SUBLANES, D_MODEL), F32)
        abuf[0:SCAN_PAD, :] = jnp.ones((SCAN_PAD, D_MODEL), F32)
        sbuf[0:SCAN_PAD, :] = jnp.zeros((SCAN_PAD, D_MODEL), F32)
        kmt[...] = jnp.zeros(kmt.shape, F32)

    xb = x_ref[...].astype(BF16)

    def proj(j):
        return jnp.dot(xb, w_ref[:, j * D_MODEL:(j + 1) * D_MODEL], preferred_element_type=F32)

    u = proj(0)
    ubuf[SUBLANES:SUBLANES + tt, :] = u
    uc = cb_ref[...] + cw_ref[CONV_W - 1:CONV_W, :] * u
    for j in range(CONV_W - 1):
        s = SUBLANES - (CONV_W - 1) + j
        uc = uc + cw_ref[j:j + 1, :] * ubuf[s:s + tt, :]
    cs_ref[...] = ubuf[SUBLANES + tt - (CONV_W - 1):SUBLANES + tt, :]
    ubuf[0:SUBLANES, :] = ubuf[tt:tt + SUBLANES, :]

    h0 = sbuf[SCAN_PAD - 1:SCAN_PAD, :]
    neg_c_sp = -_softplus(-lam_ref[...])
    ucb = uc.astype(BF16)
    for n in range(RNN_BLOCKS):
        sl = slice(n * RNN_BW, (n + 1) * RNN_BW)
        a, uin = _rglru_block(uc[:, sl], ucb[:, sl], wg_ref[n], ba_ref[:, sl], bx_ref[:, sl],
                              neg_c_sp[:, sl])
        abuf[SCAN_PAD:SCAN_PAD + tt, sl] = a
        sbuf[SCAN_PAD:SCAN_PAD + tt, sl] = uin
    sbuf[SCAN_PAD - 1:SCAN_PAD, :] = jnp.zeros((1, D_MODEL), F32)

    shift = 1
    while shift < tt:
        a_cur = abuf[SCAN_PAD:SCAN_PAD + tt, :]
        s_cur = sbuf[SCAN_PAD:SCAN_PAD + tt, :]
        a_prev = abuf[SCAN_PAD - shift:SCAN_PAD - shift + tt, :]
        s_prev = sbuf[SCAN_PAD - shift:SCAN_PAD - shift + tt, :]
        sbuf[SCAN_PAD:SCAN_PAD + tt, :] = s_cur + a_cur * s_prev
        abuf[SCAN_PAD:SCAN_PAD + tt, :] = a_cur * a_prev
        shift *= 2
    hs = sbuf[SCAN_PAD:SCAN_PAD + tt, :] + abuf[SCAN_PAD:SCAN_PAD + tt, :] * h0
    h_last = hs[tt - 1:tt, :]
    h_ref[...] = h_last
    sbuf[SCAN_PAD - 1:SCAN_PAD, :] = h_last

    g = proj(1)
    rnn_ref[...] = (hs * _gelu_tanh(g)).astype(BF16)

    qf = proj(2)
    q_ref[...] = qf.astype(BF16)
    q_hi, q_lo = _split_bf16(qf)
    km_hi, km_lo = _split_bf16(kmt[...])
    gate = (lax.dot_general(q_hi, km_hi, NT_DIMS, preferred_element_type=F32)
            + lax.dot_general(q_lo, km_hi, NT_DIMS, preferred_element_type=F32)
            + lax.dot_general(q_hi, km_lo, NT_DIMS, preferred_element_type=F32))
    blk_of_lane = lax.shift_right_logical(lax.broadcasted_iota(jnp.int32, gate.shape, 1), 3)
    cnt = jnp.zeros(gate.shape, jnp.int32)
    for d in list(range(-(N_HEADS - 1), 0)) + list(range(1, N_HEADS)):
        other = pltpu.roll(gate, (-N_HEADS * d) % LANES, 1)
        nb2 = blk_of_lane + d
        beats = (other >= gate) if d < 0 else (other > gate)
        hit = jnp.where(nb2 < i, jnp.where(beats, 1, 0), 0)
        cnt = cnt + jnp.where(nb2 >= 0, hit, 0)
    sel_ref[...] = jnp.where(blk_of_lane < i, jnp.where(cnt < MOBA_TOPK, 1.0, 0.0), 0.0).astype(F32)

    kf = proj(3)
    k_ref[...] = kf
    kb_ref[...] = kf.astype(BF16)
    kmean = jnp.sum(kf, axis=0, keepdims=True) * (1.0 / MOBA_BLOCK)
    head_of_lane = lax.broadcasted_iota(jnp.int32, (N_HEADS, D_MODEL), 1) // HEAD_DIM
    row = lax.broadcasted_iota(jnp.int32, (N_HEADS, D_MODEL), 0)
    kmt[pl.ds(pl.multiple_of(i * N_HEADS, N_HEADS), N_HEADS), :] = jnp.where(
        head_of_lane == row, jnp.broadcast_to(kmean, (N_HEADS, D_MODEL)), 0.0)

    vf = proj(4)
    v_ref[...] = vf
    vb_ref[...] = vf.astype(BF16)
    sga_ref[...] = _sigmoid(proj(5))
    sgb_ref[...] = _sigmoid(proj(6))


def _const_spec(shape):
    nd = len(shape)
    return pl.BlockSpec(shape, lambda *_: (0,) * nd, pipeline_mode=pl.Buffered(1))


def _front_prompt(x, w_in, conv_w, conv_b, w_gate, b_a, b_x, lam):
    b, t, _ = x.shape
    tt = MOBA_BLOCK
    nt = t // tt
    tile = lambda w: pl.BlockSpec((None, tt, w), lambda bi, ti: (bi, ti, 0))
    big = lambda dt: jax.ShapeDtypeStruct((b, t, D_MODEL), dt)
    out_shape = (big(BF16), big(BF16), big(F32), big(F32), big(BF16), big(BF16), big(F32), big(F32),
                 jax.ShapeDtypeStruct((b, t, LANES), F32),
                 jax.ShapeDtypeStruct((b, 1, D_MODEL), F32),
                 jax.ShapeDtypeStruct((b, CONV_W - 1, D_MODEL), F32))
    out_specs = tuple([tile(D_MODEL)] * 8 + [tile(LANES)]
                      + [pl.BlockSpec((None, 1, D_MODEL), lambda bi, ti: (bi, 0, 0)),
                         pl.BlockSpec((None, CONV_W - 1, D_MODEL), lambda bi, ti: (bi, 0, 0))])
    in_specs = [tile(D_MODEL), _const_spec(w_in.shape), _const_spec(conv_w.shape), _const_spec(conv_b.shape),
                _const_spec(w_gate.shape), _const_spec(b_a.shape), _const_spec(b_x.shape), _const_spec(lam.shape)]
    return pl.pallas_call(
        _front_kernel,
        grid=(b, nt),
        in_specs=in_specs,
        out_specs=out_specs,
        out_shape=out_shape,
        scratch_shapes=[pltpu.VMEM((SUBLANES + tt, D_MODEL), F32),
                        pltpu.VMEM((SCAN_PAD + tt, D_MODEL), F32),
                        pltpu.VMEM((SCAN_PAD + tt, D_MODEL), F32),
                        pltpu.VMEM((LANES, D_MODEL), F32)],
        compiler_params=pltpu.CompilerParams(dimension_semantics=("arbitrary", "arbitrary"),
                                             vmem_limit_bytes=VMEM_LIMIT),
        name="front_prompt",
    )(x, w_in, conv_w, conv_b, w_gate, b_a, b_x, lam)


def _attn_kernel(q_ref, kb_ref, vb_ref, sel_ref, o_ref, m_sc, l_sc, acc_sc):
    i = pl.program_id(1)
    tq = q_ref.shape[0]
    n_past = kb_ref.shape[0] // MOBA_BLOCK - 1
    row = lax.broadcasted_iota(jnp.int32, (tq, MOBA_BLOCK), 0)
    col = lax.broadcasted_iota(jnp.int32, (tq, MOBA_BLOCK), 1)
    causal = col <= row
    own = pl.multiple_of(i * MOBA_BLOCK, MOBA_BLOCK)

    for h in range(N_HEADS):
        hs = slice(h * HEAD_DIM, (h + 1) * HEAD_DIM)
        s = lax.dot_general(q_ref[:, hs], kb_ref[pl.ds(own, MOBA_BLOCK), hs], NT_DIMS,
                            preferred_element_type=F32) * SCALE
        s = jnp.where(causal, s, NEG)
        m = jnp.max(s, axis=-1, keepdims=True)
        p = jnp.exp(s - m)
        m_sc[h] = m
        l_sc[h] = jnp.sum(p, axis=-1, keepdims=True)
        acc_sc[:, hs] = jnp.dot(p.astype(BF16), vb_ref[pl.ds(own, MOBA_BLOCK), hs],
                                preferred_element_type=F32)

    for j in range(n_past):
        @pl.when(j < i)
        def _(j=j):
            for h in range(N_HEADS):
                hs = slice(h * HEAD_DIM, (h + 1) * HEAD_DIM)
                ks = slice(j * MOBA_BLOCK, (j + 1) * MOBA_BLOCK)
                c = j * N_HEADS + h
                s = lax.dot_general(q_ref[:, hs], kb_ref[ks, hs], NT_DIMS,
                                    preferred_element_type=F32) * SCALE
                s = jnp.where(sel_ref[:, c:c + 1] > 0.5, s, NEG)
                m_prev = m_sc[h]
                m_new = jnp.maximum(m_prev, jnp.max(s, axis=-1, keepdims=True))
                alpha = jnp.exp(m_prev - m_new)
                p = jnp.exp(s - m_new)
                l_sc[h] = alpha * l_sc[h] + jnp.sum(p, axis=-1, keepdims=True)
                acc_sc[:, hs] = alpha * acc_sc[:, hs] + jnp.dot(
                    p.astype(BF16), vb_ref[ks, hs], preferred_element_type=F32)
                m_sc[h] = m_new

    for h in range(N_HEADS):
        hs = slice(h * HEAD_DIM, (h + 1) * HEAD_DIM)
        o_ref[:, hs] = (acc_sc[:, hs] / l_sc[h]).astype(BF16)


def _attn_prompt(q, kb, vb, sel):
    b, t, _ = q.shape
    tq = MOBA_BLOCK
    nq = t // tq
    tile = lambda w: pl.BlockSpec((None, tq, w), lambda bi, qi: (bi, qi, 0))
    full = pl.BlockSpec((None, t, D_MODEL), lambda bi, qi: (bi, 0, 0))
    return pl.pallas_call(
        _attn_kernel,
        grid=(b, nq),
        in_specs=[tile(D_MODEL), full, full, tile(LANES)],
        out_specs=tile(D_MODEL),
        out_shape=jax.ShapeDtypeStruct((b, t, D_MODEL), BF16),
        scratch_shapes=[pltpu.VMEM((N_HEADS, tq, 1), F32),
                        pltpu.VMEM((N_HEADS, tq, 1), F32),
                        pltpu.VMEM((tq, D_MODEL), F32)],
        compiler_params=pltpu.CompilerParams(dimension_semantics=("arbitrary", "arbitrary"),
                                             vmem_limit_bytes=VMEM_LIMIT),
        name="attn_prompt",
    )(q, kb, vb, sel)


def _post_kernel(rnn_ref, o_ref, sga_ref, sgb_ref, x_ref, wbr_ref, wba_ref, wout_ref, wup_ref, wdn_ref,
                 g1_ref, b1_ref, bup_ref, bdn_ref, g2_ref, b2_ref, y_ref):
    y_rnn = jnp.dot(rnn_ref[...], wbr_ref[...], preferred_element_type=F32)
    y_att = jnp.dot(o_ref[...], wba_ref[...], preferred_element_type=F32)
    mix = sga_ref[...] * y_rnn + sgb_ref[...] * y_att
    t1 = ALPHA * x_ref[...] + jnp.dot(mix.astype(BF16), wout_ref[...], preferred_element_type=F32)
    x1 = _layer_norm(t1, g1_ref[...], b1_ref[...])
    x1b = x1.astype(BF16)
    acc = ALPHA * x1 + bdn_ref[...]
    n_chunks = D_FF // D_MODEL
    for c in range(n_chunks):
        cs = slice(c * D_MODEL, (c + 1) * D_MODEL)
        hid = jnp.maximum(jnp.dot(x1b, wup_ref[:, cs], preferred_element_type=F32) + bup_ref[:, cs], 0.0)
        acc = acc + jnp.dot((hid * hid).astype(BF16), wdn_ref[cs, :], preferred_element_type=F32)
    y_ref[...] = _layer_norm(acc, g2_ref[...], b2_ref[...])


def _post(rnn, o, sga, sgb, x, wbr, wba, wout, wup, wdn, g1, b1, bup, bdn, g2, b2, tm):
    m = x.shape[0]
    tile = pl.BlockSpec((tm, D_MODEL), lambda r: (r, 0))
    consts = (wbr, wba, wout, wup, wdn, g1, b1, bup, bdn, g2, b2)
    return pl.pallas_call(
        _post_kernel,
        grid=(m // tm,),
        in_specs=[tile] * 5 + [_const_spec(c.shape) for c in consts],
        out_specs=tile,
        out_shape=jax.ShapeDtypeStruct((m, D_MODEL), F32),
        compiler_params=pltpu.CompilerParams(dimension_semantics=("arbitrary",),
                                             vmem_limit_bytes=VMEM_LIMIT),
        name="post_m%d" % m,
    )(rnn, o, sga, sgb, x, *consts)


def _front_sample_kernel(x_ref, p0_ref, p1_ref, p2_ref, h0_ref, w_ref, cw_ref, cb_ref, wg_ref, ba_ref,
                         bx_ref, lam_ref,
                         rnn_ref, q_ref, k_ref, v_ref, sga_ref, sgb_ref, h_ref, u_ref):
    xb = x_ref[...].astype(BF16)

    def proj(j):
        return jnp.dot(xb, w_ref[:, j * D_MODEL:(j + 1) * D_MODEL], preferred_element_type=F32)

    u = proj(0)
    u_ref[...] = u
    uc = (cb_ref[...] + cw_ref[0:1, :] * p0_ref[...] + cw_ref[1:2, :] * p1_ref[...]
          + cw_ref[2:3, :] * p2_ref[...] + cw_ref[3:4, :] * u)
    neg_c_sp = -_softplus(-lam_ref[...])
    ucb = uc.astype(BF16)
    g = proj(1)
    for n in range(RNN_BLOCKS):
        sl = slice(n * RNN_BW, (n + 1) * RNN_BW)
        a, uin = _rglru_block(uc[:, sl], ucb[:, sl], wg_ref[n], ba_ref[:, sl], bx_ref[:, sl],
                              neg_c_sp[:, sl])
        h = a * h0_ref[:, sl] + uin
        h_ref[:, sl] = h
        rnn_ref[:, sl] = (h * _gelu_tanh(g[:, sl])).astype(BF16)
    q_ref[...] = proj(2)
    k_ref[...] = proj(3)
    v_ref[...] = proj(4)
    sga_ref[...] = _sigmoid(proj(5))
    sgb_ref[...] = _sigmoid(proj(6))


def _front_sample(x, p0, p1, p2, h0, w_in, conv_w, conv_b, w_gate, b_a, b_x, lam):
    m = x.shape[0]
    args = (x, p0, p1, p2, h0, w_in, conv_w, conv_b, w_gate, b_a, b_x, lam)
    row = lambda dt: jax.ShapeDtypeStruct((m, D_MODEL), dt)
    return pl.pallas_call(
        _front_sample_kernel,
        grid=(1,),
        in_specs=[_const_spec(a.shape) for a in args],
        out_specs=tuple(pl.BlockSpec((m, D_MODEL), lambda r: (0, 0)) for _ in range(8)),
        out_shape=(row(BF16), row(F32), row(F32), row(F32), row(F32), row(F32), row(F32), row(F32)),
        compiler_params=pltpu.CompilerParams(dimension_semantics=("arbitrary",),
                                             vmem_limit_bytes=VMEM_LIMIT),
        name="front_sample",
    )(*args)


def _sample_attn_kernel(pt_ref, q_ref, kn_ref, vn_ref, ck_ref, cv_ref, o_ref,
                        pagebuf, psem, ksum, kbuf, vbuf, ksem, vsem):
    b = pl.program_id(0)
    n_seq = pl.num_programs(0)
    n_pages = pt_ref.shape[1]
    n_blocks = n_pages // PAGES_PER_BLOCK
    n_gather = MOBA_TOPK * PAGES_PER_BLOCK

    def page_copy(seq, pg, slot):
        return pltpu.make_async_copy(ck_ref.at[0, pt_ref[seq, pg]], pagebuf.at[slot], psem.at[slot])

    @pl.when(b == 0)
    def _():
        for s in range(PAGE_BUFS):
            page_copy(0, s, s).start()

    def block_sum(n, carry):
        acc = jnp.zeros((N_HEADS, HEAD_DIM), F32)
        for t in range(PAGES_PER_BLOCK):
            pg = n * PAGES_PER_BLOCK + t
            slot = lax.rem(pg, PAGE_BUFS)
            page_copy(b, pg, slot).wait()
            acc = acc + jnp.sum(pagebuf[slot], axis=0)
            nxt = b * n_pages + pg + PAGE_BUFS

            @pl.when(nxt < n_seq * n_pages)
            def _():
                page_copy(nxt // n_pages, lax.rem(nxt, n_pages), slot).start()
        ksum[n] = acc
        return carry

    lax.fori_loop(0, n_blocks, block_sum, 0)

    q = q_ref[...]
    gate = jnp.sum(ksum[...] * (1.0 / MOBA_BLOCK) * q[None], axis=-1, keepdims=True)
    blk = lax.broadcasted_iota(jnp.int32, gate.shape, 0)

    def gather_copy(src_ref, dst_ref, sem, page, h, slot):
        return pltpu.make_async_copy(src_ref.at[0, page, :, h, :], dst_ref.at[h, slot], sem)

    for t in range(MOBA_TOPK):
        mx = jnp.max(gate, axis=0, keepdims=True)
        idx = jnp.min(jnp.where(gate == mx, blk, n_blocks), axis=0, keepdims=True)
        gate = jnp.where(blk == idx, -jnp.inf, gate)
        for h in range(N_HEADS):
            sel = idx[0, h, 0]
            for pg in range(PAGES_PER_BLOCK):
                page = pt_ref[b, sel * PAGES_PER_BLOCK + pg]
                slot = t * PAGES_PER_BLOCK + pg
                gather_copy(ck_ref, kbuf, ksem, page, h, slot).start()
                gather_copy(cv_ref, vbuf, vsem, page, h, slot).start()

    for h in range(N_HEADS):
        for slot in range(n_gather):
            gather_copy(ck_ref, kbuf, ksem, 0, h, slot).wait()
            gather_copy(cv_ref, vbuf, vsem, 0, h, slot).wait()

    for h in range(N_HEADS):
        qh = q[h:h + 1, :]
        kh = kbuf[h].reshape(n_gather * PAGE_SIZE, HEAD_DIM)
        vh = vbuf[h].reshape(n_gather * PAGE_SIZE, HEAD_DIM)
        s = jnp.sum(kh * qh, axis=-1, keepdims=True) * SCALE
        s_new = jnp.sum(kn_ref[h:h + 1, :] * qh, axis=-1, keepdims=True) * SCALE
        m = jnp.maximum(jnp.max(s, axis=0, keepdims=True), s_new)
        p = jnp.exp(s - m)
        p_new = jnp.exp(s_new - m)
        l = jnp.sum(p, axis=0, keepdims=True) + p_new
        o = jnp.sum(p * vh, axis=0, keepdims=True) + p_new * vn_ref[h:h + 1, :]
        o_ref[h:h + 1, :] = o / l


def _sample_attn(page_table, q, k_new, v_new, cache_k, cache_v):
    n_seq, n_pages = page_table.shape
    assert (n_pages * PAGE_SIZE) % MOBA_BLOCK == 0, "own-block cached prefix is not supported"
    assert n_pages % PAGE_BUFS == 0 and n_pages // PAGES_PER_BLOCK >= MOBA_TOPK
    n_blocks = n_pages // PAGES_PER_BLOCK
    n_gather = MOBA_TOPK * PAGES_PER_BLOCK
    row = pl.BlockSpec((None, N_HEADS, HEAD_DIM), lambda s, pt: (s, 0, 0))
    anyspec = pl.BlockSpec(memory_space=pl.ANY)
    grid_spec = pltpu.PrefetchScalarGridSpec(
        num_scalar_prefetch=1,
        grid=(n_seq,),
        in_specs=[row, row, row, anyspec, anyspec],
        out_specs=row,
        scratch_shapes=[pltpu.VMEM((PAGE_BUFS, PAGE_SIZE, N_HEADS, HEAD_DIM), F32),
                        pltpu.SemaphoreType.DMA((PAGE_BUFS,)),
                        pltpu.VMEM((n_blocks, N_HEADS, HEAD_DIM), F32),
                        pltpu.VMEM((N_HEADS, n_gather, PAGE_SIZE, HEAD_DIM), F32),
                        pltpu.VMEM((N_HEADS, n_gather, PAGE_SIZE, HEAD_DIM), F32),
                        pltpu.SemaphoreType.DMA(()),
                        pltpu.SemaphoreType.DMA(())])
    return pl.pallas_call(
        _sample_attn_kernel,
        grid_spec=grid_spec,
        out_shape=jax.ShapeDtypeStruct((n_seq, N_HEADS, HEAD_DIM), F32),
        compiler_params=pltpu.CompilerParams(dimension_semantics=("arbitrary",),
                                             vmem_limit_bytes=VMEM_LIMIT),
        name="attn_sample",
    )(page_table, q, k_new, v_new, cache_k, cache_v)


def kernel(x_prompt, x_sample, cache_k, cache_v, state_h, state_conv, page_table, w_in, conv_w, conv_b,
           w_rg_a, b_rg_a, w_rg_x, b_rg_x, lru_lambda, w_br_rnn, w_br_attn, w_out, ln1_g, ln1_b,
           w_up, b_up, w_down, b_down, ln2_g, ln2_b):
    assert w_in.shape[0] == 1, "single-layer trunk"
    b, t, _ = x_prompt.shape
    db = x_sample.shape[0]
    assert x_sample.shape[1] == 1 and t % MOBA_BLOCK == 0

    w_in_b = w_in[0].astype(BF16)
    w_gate = jnp.concatenate([w_rg_a[0], w_rg_x[0]], axis=-1).astype(BF16)
    cw, cb = conv_w[0], conv_b
    post_w = (w_br_rnn[0].astype(BF16), w_br_attn[0].astype(BF16), w_out[0].astype(BF16),
              w_up[0].astype(BF16), w_down[0].astype(BF16), ln1_g, ln1_b, b_up, b_down, ln2_g, ln2_b)

    (rnn_p, q_p, k_p, v_p, kb_p, vb_p, sga_p, sgb_p, sel_p, h_p, cs_p) = _front_prompt(
        x_prompt, w_in_b, cw, cb, w_gate, b_rg_a, b_rg_x, lru_lambda)
    o_p = _attn_prompt(q_p, kb_p, vb_p, sel_p)
    flat = lambda a: a.reshape(b * t, D_MODEL)
    y_p = _post(flat(rnn_p), flat(o_p), flat(sga_p), flat(sgb_p), flat(x_prompt), *post_w, tm=256)

    xs = x_sample.reshape(db, D_MODEL)
    sc = state_conv[0]
    (rnn_s, q_s, k_s, v_s, sga_s, sgb_s, h_s, u_s) = _front_sample(
        xs, sc[:, 0], sc[:, 1], sc[:, 2], state_h[0], w_in_b, cw, cb, w_gate, b_rg_a, b_rg_x, lru_lambda)
    heads = lambda a: a.reshape(db, N_HEADS, HEAD_DIM)
    o_s = _sample_attn(page_table, heads(q_s), heads(k_s), heads(v_s), cache_k, cache_v)
    y_s = _post(rnn_s, o_s.reshape(db, D_MODEL).astype(BF16), sga_s, sgb_s, xs, *post_w, tm=db)
    cs_s = jnp.concatenate([sc[:, 1:], u_s[:, None, :]], axis=1)

    kv_p = lambda a: a.reshape(1, b, t, N_HEADS, HEAD_DIM)
    kv_s = lambda a: a.reshape(1, db, 1, N_HEADS, HEAD_DIM)
    return (y_p.reshape(b, t, D_MODEL), y_s.reshape(db, 1, D_MODEL), kv_p(k_p), kv_p(v_p),
            h_p.reshape(1, b, D_MODEL), cs_p[None],
            kv_s(k_s), kv_s(v_s), h_s[None], cs_s[None])
```

```python
import math

import jax
import jax.numpy as jnp
from jax import lax
from jax.experimental import pallas as pl
from jax.experimental.pallas import tpu as pltpu

F32 = jnp.float32
BF16 = jnp.bfloat16

D_MODEL = 1024
N_HEADS = 8
HEAD_DIM = 128
RNN_BLOCKS = 8
RNN_BW = 128
CONV_W = 4
LRU_C = 8.0
MOBA_BLOCK = 256
MOBA_TOPK = 3
PAGE_SIZE = 128
PAGES_PER_BLOCK = MOBA_BLOCK // PAGE_SIZE
D_FF = 4096
ALPHA = 2.0 ** 0.25
LN_EPS = 1e-5
NEG = -1e30
SCALE = HEAD_DIM ** -0.5
LOG2E = math.log2(math.e)

SUBLANES = 8
VMEM_LIMIT = 56 * 1024 * 1024
PAGE_BUFS = 32


def _sigmoid(x):
    return 0.5 * jnp.tanh(0.5 * x) + 0.5


def _softplus(x):
    return jnp.maximum(x, 0.0) + jnp.log1p(jnp.exp(-jnp.abs(x)))


def _gelu_tanh(x):
    c = math.sqrt(2.0 / math.pi)
    return x * (0.5 * (1.0 + jnp.tanh(c * (x + 0.044715 * (x * x * x)))))


def _layer_norm(x, g, b):
    mu = jnp.mean(x, axis=-1, keepdims=True)
    xc = x - mu
    var = jnp.mean(xc * xc, axis=-1, keepdims=True)
    return xc * lax.rsqrt(var + LN_EPS) * g + b


def _split_bf16(x):
    hi = x.astype(BF16)
    lo = (x - hi.astype(F32)).astype(BF16)
    return hi, lo


def _rglru_block(uc_blk, ucb_blk, wg, ba, bx, neg_c_sp):
    gz = jnp.dot(ucb_blk, wg, preferred_element_type=F32)
    r = _sigmoid(gz[:, :RNN_BW] + ba)
    ig = _sigmoid(gz[:, RNN_BW:] + bx)
    log_a = (LRU_C * r) * neg_c_sp
    a = jnp.exp(log_a)
    t = jnp.tanh(log_a)
    mult = jnp.sqrt(-2.0 * t / (1.0 - t))
    return a, mult * (ig * uc_blk)


def _front_kernel(x_ref, w_ref, cw_ref, cb_ref, wg_ref, ba_ref, bx_ref, lam_ref,
                  rnn_ref, qt_ref, k_ref, v_ref, kb_ref, vt_ref, sga_ref, sgb_ref, selt_ref,
                  h_ref, cs_ref,
                  ubuf, abuf, sbuf, hbuf, hcar, kmt):
    i = pl.program_id(1)
    tt = x_ref.shape[0]
    n_blk = kmt.shape[0] // N_HEADS

    @pl.when(i == 0)
    def _():
        ubuf[0:SUBLANES, :] = jnp.zeros((SUBLANES, D_MODEL), F32)
        hcar[...] = jnp.zeros(hcar.shape, F32)
        kmt[...] = jnp.zeros(kmt.shape, F32)

    xb = x_ref[...].astype(BF16)

    def proj(j):
        return jnp.dot(xb, w_ref[:, j * D_MODEL:(j + 1) * D_MODEL], preferred_element_type=F32)

    u = proj(0)
    ubuf[SUBLANES:SUBLANES + tt, :] = u
    uc = cb_ref[...] + cw_ref[CONV_W - 1:CONV_W, :] * u
    for j in range(CONV_W - 1):
        s = SUBLANES - (CONV_W - 1) + j
        uc = uc + cw_ref[j:j + 1, :] * ubuf[s:s + tt, :]
    cs_ref[...] = ubuf[SUBLANES + tt - (CONV_W - 1):SUBLANES + tt, :]
    ubuf[0:SUBLANES, :] = ubuf[tt:tt + SUBLANES, :]

    neg_c_sp = -_softplus(-lam_ref[...])
    ucb = uc.astype(BF16)
    for n in range(RNN_BLOCKS):
        sl = slice(n * RNN_BW, (n + 1) * RNN_BW)
        a, uin = _rglru_block(uc[:, sl], ucb[:, sl], wg_ref[n], ba_ref[:, sl], bx_ref[:, sl],
                              neg_c_sp[:, sl])
        abuf[:, sl] = a
        sbuf[:, sl] = uin

    sub = lax.broadcasted_iota(jnp.int32, (SUBLANES, D_MODEL), 0)
    h = hcar[0:1, :]
    for r in range(tt // SUBLANES):
        rows = slice(r * SUBLANES, (r + 1) * SUBLANES)
        a = abuf[rows, :]
        s = sbuf[rows, :]
        for sh in (1, 2, 4):
            a_prev = jnp.where(sub >= sh, pltpu.roll(a, sh, 0), 1.0)
            s_prev = jnp.where(sub >= sh, pltpu.roll(s, sh, 0), 0.0)
            s = s + a * s_prev
            a = a * a_prev
        hs_r = s + a * h
        hbuf[rows, :] = hs_r
        h = hs_r[SUBLANES - 1:SUBLANES, :]
    hcar[0:1, :] = h
    h_ref[...] = h

    g = proj(1)
    rnn_ref[...] = (hbuf[...] * _gelu_tanh(g)).astype(BF16)

    qt = proj(2).T
    qt_ref[...] = (qt * (SCALE * LOG2E)).astype(BF16)
    q_hi, q_lo = _split_bf16(qt)
    km_hi, km_lo = _split_bf16(kmt[...])
    nk = kmt.shape[0]
    g2 = jnp.dot(jnp.concatenate([km_hi, km_lo], axis=0), q_hi, preferred_element_type=F32)
    gate = g2[:nk] + g2[nk:] + jnp.dot(km_hi, q_lo, preferred_element_type=F32)
    gn = [gate[n * N_HEADS:(n + 1) * N_HEADS, :] for n in range(n_blk)]
    for n in range(n_blk):
        cnt = jnp.zeros(gn[n].shape, jnp.int32)
        for m in range(n_blk):
            if m == n:
                continue
            beats = (gn[m] >= gn[n]) if m < n else (gn[m] > gn[n])
            cnt = cnt + jnp.where(beats, 1, 0) * jnp.where(m < i, 1, 0)
        keep = jnp.where(cnt < MOBA_TOPK, 1.0, 0.0) * jnp.where(n < i, 1.0, 0.0)
        selt_ref[n * N_HEADS:(n + 1) * N_HEADS, :] = keep.astype(F32)

    kf = proj(3)
    k_ref[...] = kf
    kb_ref[...] = kf.astype(BF16)
    kmean = jnp.sum(kf, axis=0, keepdims=True) * (1.0 / MOBA_BLOCK)
    head_of_lane = lax.broadcasted_iota(jnp.int32, (N_HEADS, D_MODEL), 1) // HEAD_DIM
    row = lax.broadcasted_iota(jnp.int32, (N_HEADS, D_MODEL), 0)
    kmt[pl.ds(pl.multiple_of(i * N_HEADS, N_HEADS), N_HEADS), :] = jnp.where(
        head_of_lane == row, jnp.broadcast_to(kmean, (N_HEADS, D_MODEL)), 0.0)

    vf = proj(4)
    v_ref[...] = vf
    vt_ref[...] = vf.T.astype(BF16)
    sga_ref[...] = _sigmoid(proj(5))
    sgb_ref[...] = _sigmoid(proj(6))


def _const_spec(shape):
    nd = len(shape)
    return pl.BlockSpec(shape, lambda *_: (0,) * nd, pipeline_mode=pl.Buffered(1))


def _front_prompt(x, w_in, conv_w, conv_b, w_gate, b_a, b_x, lam):
    b, t, _ = x.shape
    tt = MOBA_BLOCK
    nt = t // tt
    tile = pl.BlockSpec((None, tt, D_MODEL), lambda bi, ti: (bi, ti, 0))
    ttile = pl.BlockSpec((None, D_MODEL, tt), lambda bi, ti: (bi, 0, ti))
    big = lambda dt: jax.ShapeDtypeStruct((b, t, D_MODEL), dt)
    tbig = jax.ShapeDtypeStruct((b, D_MODEL, t), BF16)
    out_shape = (big(BF16), tbig, big(F32), big(F32), big(BF16), tbig, big(F32), big(F32),
                 jax.ShapeDtypeStruct((b, nt, nt * N_HEADS, tt), F32),
                 jax.ShapeDtypeStruct((b, 1, D_MODEL), F32),
                 jax.ShapeDtypeStruct((b, CONV_W - 1, D_MODEL), F32))
    out_specs = (tile, ttile, tile, tile, tile, ttile, tile, tile,
                 pl.BlockSpec((None, None, nt * N_HEADS, tt), lambda bi, ti: (bi, ti, 0, 0)),
                 pl.BlockSpec((None, 1, D_MODEL), lambda bi, ti: (bi, 0, 0)),
                 pl.BlockSpec((None, CONV_W - 1, D_MODEL), lambda bi, ti: (bi, 0, 0)))
    in_specs = [tile, _const_spec(w_in.shape), _const_spec(conv_w.shape), _const_spec(conv_b.shape),
                _const_spec(w_gate.shape), _const_spec(b_a.shape), _const_spec(b_x.shape), _const_spec(lam.shape)]
    return pl.pallas_call(
        _front_kernel,
        grid=(b, nt),
        in_specs=in_specs,
        out_specs=out_specs,
        out_shape=out_shape,
        scratch_shapes=[pltpu.VMEM((SUBLANES + tt, D_MODEL), F32),
                        pltpu.VMEM((tt, D_MODEL), F32),
                        pltpu.VMEM((tt, D_MODEL), F32),
                        pltpu.VMEM((tt, D_MODEL), F32),
                        pltpu.VMEM((SUBLANES, D_MODEL), F32),
                        pltpu.VMEM((nt * N_HEADS, D_MODEL), F32)],
        compiler_params=pltpu.CompilerParams(dimension_semantics=("arbitrary", "arbitrary"),
                                             vmem_limit_bytes=VMEM_LIMIT),
        name="front_prompt",
    )(x, w_in, conv_w, conv_b, w_gate, b_a, b_x, lam)


def _attn_kernel(qt_ref, kb_ref, vt_ref, selt_ref, o_ref):
    i = pl.program_id(1)
    tq = qt_ref.shape[1]
    n_q = kb_ref.shape[0] // MOBA_BLOCK
    key_idx = lax.broadcasted_iota(jnp.int32, (MOBA_BLOCK, tq), 0)
    qry_idx = lax.broadcasted_iota(jnp.int32, (MOBA_BLOCK, tq), 1)
    causal = key_idx <= qry_idx

    def attend(n_past):
        nk = (n_past + 1) * MOBA_BLOCK
        for h in range(N_HEADS):
            hs = slice(h * HEAD_DIM, (h + 1) * HEAD_DIM)
            s = jnp.dot(kb_ref[0:nk, hs], qt_ref[hs, :], preferred_element_type=F32)
            parts = []
            for j in range(n_past):
                c = j * N_HEADS + h
                parts.append(jnp.where(selt_ref[c:c + 1, :] > 0.5,
                                       s[j * MOBA_BLOCK:(j + 1) * MOBA_BLOCK, :], NEG))
            parts.append(jnp.where(causal, s[n_past * MOBA_BLOCK:, :], NEG))
            s = jnp.concatenate(parts, axis=0) if n_past else parts[0]
            m = jnp.max(s, axis=0, keepdims=True)
            p = jnp.exp2(s - m)
            l = jnp.sum(p, axis=0, keepdims=True)
            ot = jnp.dot(vt_ref[hs, 0:nk], p.astype(BF16), preferred_element_type=F32) * (1.0 / l)
            o_ref[:, hs] = ot.T.astype(BF16)

    for v in range(n_q):
        @pl.when(i == v)
        def _(v=v):
            attend(v)


def _attn_prompt(qt, kb, vt, selt):
    b, _, t = qt.shape
    tq = MOBA_BLOCK
    nq = t // tq
    return pl.pallas_call(
        _attn_kernel,
        grid=(b, nq),
        in_specs=[pl.BlockSpec((None, D_MODEL, tq), lambda bi, qi: (bi, 0, qi)),
                  pl.BlockSpec((None, t, D_MODEL), lambda bi, qi: (bi, 0, 0)),
                  pl.BlockSpec((None, D_MODEL, t), lambda bi, qi: (bi, 0, 0)),
                  pl.BlockSpec((None, None, nq * N_HEADS, tq), lambda bi, qi: (bi, qi, 0, 0))],
        out_specs=pl.BlockSpec((None, tq, D_MODEL), lambda bi, qi: (bi, qi, 0)),
        out_shape=jax.ShapeDtypeStruct((b, t, D_MODEL), BF16),
        compiler_params=pltpu.CompilerParams(dimension_semantics=("arbitrary", "arbitrary"),
                                             vmem_limit_bytes=VMEM_LIMIT),
        name="attn_prompt",
    )(qt, kb, vt, selt)


def _post_kernel(rnn_ref, o_ref, sga_ref, sgb_ref, x_ref, wbr_ref, wba_ref, wout_ref, wup_ref, wdn_ref,
                 g1_ref, b1_ref, bup_ref, bdn_ref, g2_ref, b2_ref, y_ref):
    y_rnn = jnp.dot(rnn_ref[...], wbr_ref[...], preferred_element_type=F32)
    y_att = jnp.dot(o_ref[...], wba_ref[...], preferred_element_type=F32)
    mix = sga_ref[...] * y_rnn + sgb_ref[...] * y_att
    t1 = ALPHA * x_ref[...] + jnp.dot(mix.astype(BF16), wout_ref[...], preferred_element_type=F32)
    x1 = _layer_norm(t1, g1_ref[...], b1_ref[...])
    x1b = x1.astype(BF16)
    acc = ALPHA * x1 + bdn_ref[...]
    n_chunks = D_FF // D_MODEL
    for c in range(n_chunks):
        cs = slice(c * D_MODEL, (c + 1) * D_MODEL)
        hid = jnp.maximum(jnp.dot(x1b, wup_ref[:, cs], preferred_element_type=F32) + bup_ref[:, cs], 0.0)
        acc = acc + jnp.dot((hid * hid).astype(BF16), wdn_ref[cs, :], preferred_element_type=F32)
    y_ref[...] = _layer_norm(acc, g2_ref[...], b2_ref[...])


def _post(rnn, o, sga, sgb, x, wbr, wba, wout, wup, wdn, g1, b1, bup, bdn, g2, b2, tm):
    m = x.shape[0]
    tile = pl.BlockSpec((tm, D_MODEL), lambda r: (r, 0))
    consts = (wbr, wba, wout, wup, wdn, g1, b1, bup, bdn, g2, b2)
    return pl.pallas_call(
        _post_kernel,
        grid=(m // tm,),
        in_specs=[tile] * 5 + [_const_spec(c.shape) for c in consts],
        out_specs=tile,
        out_shape=jax.ShapeDtypeStruct((m, D_MODEL), F32),
        compiler_params=pltpu.CompilerParams(dimension_semantics=("arbitrary",),
                                             vmem_limit_bytes=VMEM_LIMIT),
        name="post_m%d" % m,
    )(rnn, o, sga, sgb, x, *consts)


def _front_sample_kernel(x_ref, p0_ref, p1_ref, p2_ref, h0_ref, w_ref, cw_ref, cb_ref, wg_ref, ba_ref,
                         bx_ref, lam_ref,
                         rnn_ref, q_ref, k_ref, v_ref, sga_ref, sgb_ref, h_ref, u_ref):
    xb = x_ref[...].astype(BF16)

    def proj(j):
        return jnp.dot(xb, w_ref[:, j * D_MODEL:(j + 1) * D_MODEL], preferred_element_type=F32)

    u = proj(0)
    u_ref[...] = u
    uc = (cb_ref[...] + cw_ref[0:1, :] * p0_ref[...] + cw_ref[1:2, :] * p1_ref[...]
          + cw_ref[2:3, :] * p2_ref[...] + cw_ref[3:4, :] * u)
    neg_c_sp = -_softplus(-lam_ref[...])
    ucb = uc.astype(BF16)
    g = proj(1)
    for n in range(RNN_BLOCKS):
        sl = slice(n * RNN_BW, (n + 1) * RNN_BW)
        a, uin = _rglru_block(uc[:, sl], ucb[:, sl], wg_ref[n], ba_ref[:, sl], bx_ref[:, sl],
                              neg_c_sp[:, sl])
        h = a * h0_ref[:, sl] + uin
        h_ref[:, sl] = h
        rnn_ref[:, sl] = (h * _gelu_tanh(g[:, sl])).astype(BF16)
    q_ref[...] = proj(2)
    k_ref[...] = proj(3)
    v_ref[...] = proj(4)
    sga_ref[...] = _sigmoid(proj(5))
    sgb_ref[...] = _sigmoid(proj(6))


def _front_sample(x, p0, p1, p2, h0, w_in, conv_w, conv_b, w_gate, b_a, b_x, lam):
    m = x.shape[0]
    args = (x, p0, p1, p2, h0, w_in, conv_w, conv_b, w_gate, b_a, b_x, lam)
    row = lambda dt: jax.ShapeDtypeStruct((m, D_MODEL), dt)
    return pl.pallas_call(
        _front_sample_kernel,
        grid=(1,),
        in_specs=[_const_spec(a.shape) for a in args],
        out_specs=tuple(pl.BlockSpec((m, D_MODEL), lambda r: (0, 0)) for _ in range(8)),
        out_shape=(row(BF16), row(F32), row(F32), row(F32), row(F32), row(F32), row(F32), row(F32)),
        compiler_params=pltpu.CompilerParams(dimension_semantics=("arbitrary",),
                                             vmem_limit_bytes=VMEM_LIMIT),
        name="front_sample",
    )(*args)


def _sample_attn_kernel(pt_ref, q_ref, kn_ref, vn_ref, ck_ref, cv_ref, o_ref,
                        pagebuf, psem, ksum, kbuf, vbuf, ksem, vsem):
    b = pl.program_id(0)
    n_seq = pl.num_programs(0)
    n_pages = pt_ref.shape[1]
    n_blocks = n_pages // PAGES_PER_BLOCK
    n_gather = MOBA_TOPK * PAGES_PER_BLOCK
    par = lax.rem(b, 2)

    def page_copy(seq, pg, slot):
        return pltpu.make_async_copy(ck_ref.at[0, pt_ref[seq, pg]], pagebuf.at[slot], psem.at[slot])

    @pl.when(b == 0)
    def _():
        for s in range(PAGE_BUFS):
            page_copy(0, s, s).start()

    def block_sum(n, carry):
        acc = jnp.zeros((N_HEADS, HEAD_DIM), F32)
        for t in range(PAGES_PER_BLOCK):
            pg = n * PAGES_PER_BLOCK + t
            slot = lax.rem(pg, PAGE_BUFS)
            page_copy(b, pg, slot).wait()
            acc = acc + jnp.sum(pagebuf[slot], axis=0)
            nxt = b * n_pages + pg + PAGE_BUFS

            @pl.when(nxt < n_seq * n_pages)
            def _():
                page_copy(nxt // n_pages, lax.rem(nxt, n_pages), slot).start()
        ksum[n] = acc
        return carry

    lax.fori_loop(0, n_blocks, block_sum, 0)

    gate = jnp.sum(ksum[...] * (1.0 / MOBA_BLOCK) * q_ref[b][None], axis=-1, keepdims=True)
    blk = lax.broadcasted_iota(jnp.int32, gate.shape, 0)

    def gather_copy(src_ref, dst_ref, sem, page, h, slot, buf):
        return pltpu.make_async_copy(src_ref.at[0, page, :, h, :], dst_ref.at[buf, h, slot], sem.at[buf])

    for t in range(MOBA_TOPK):
        mx = jnp.max(gate, axis=0, keepdims=True)
        idx = jnp.min(jnp.where(gate == mx, blk, n_blocks), axis=0, keepdims=True)
        gate = jnp.where(blk == idx, -jnp.inf, gate)
        for h in range(N_HEADS):
            sel = idx[0, h, 0]
            for pg in range(PAGES_PER_BLOCK):
                page = pt_ref[b, sel * PAGES_PER_BLOCK + pg]
                slot = t * PAGES_PER_BLOCK + pg
                gather_copy(ck_ref, kbuf, ksem, page, h, slot, par).start()
                gather_copy(cv_ref, vbuf, vsem, page, h, slot, par).start()

    def attend(seq, buf):
        for h in range(N_HEADS):
            for slot in range(n_gather):
                gather_copy(ck_ref, kbuf, ksem, 0, h, slot, buf).wait()
                gather_copy(cv_ref, vbuf, vsem, 0, h, slot, buf).wait()
        q = q_ref[seq]
        kn = kn_ref[seq]
        vn = vn_ref[seq]
        for h in range(N_HEADS):
            qh = q[h:h + 1, :]
            kh = kbuf[buf, h].reshape(n_gather * PAGE_SIZE, HEAD_DIM)
            vh = vbuf[buf, h].reshape(n_gather * PAGE_SIZE, HEAD_DIM)
            s = jnp.sum(kh * qh, axis=-1, keepdims=True) * SCALE
            s_new = jnp.sum(kn[h:h + 1, :] * qh, axis=-1, keepdims=True) * SCALE
            m = jnp.maximum(jnp.max(s, axis=0, keepdims=True), s_new)
            p = jnp.exp(s - m)
            p_new = jnp.exp(s_new - m)
            l = jnp.sum(p, axis=0, keepdims=True) + p_new
            o = jnp.sum(p * vh, axis=0, keepdims=True) + p_new * vn[h:h + 1, :]
            o_ref[seq, h:h + 1, :] = o / l

    @pl.when(b > 0)
    def _():
        attend(b - 1, 1 - par)

    @pl.when(b == n_seq - 1)
    def _():
        attend(b, par)


def _sample_attn(page_table, q, k_new, v_new, cache_k, cache_v):
    n_seq, n_pages = page_table.shape
    assert (n_pages * PAGE_SIZE) % MOBA_BLOCK == 0, "own-block cached prefix is not supported"
    assert n_pages % PAGE_BUFS == 0 and n_pages // PAGES_PER_BLOCK >= MOBA_TOPK
    n_blocks = n_pages // PAGES_PER_BLOCK
    n_gather = MOBA_TOPK * PAGES_PER_BLOCK
    whole = pl.BlockSpec((n_seq, N_HEADS, HEAD_DIM), lambda s, pt: (0, 0, 0))
    anyspec = pl.BlockSpec(memory_space=pl.ANY)
    gathered = pltpu.VMEM((2, N_HEADS, n_gather, PAGE_SIZE, HEAD_DIM), F32)
    grid_spec = pltpu.PrefetchScalarGridSpec(
        num_scalar_prefetch=1,
        grid=(n_seq,),
        in_specs=[whole, whole, whole, anyspec, anyspec],
        out_specs=whole,
        scratch_shapes=[pltpu.VMEM((PAGE_BUFS, PAGE_SIZE, N_HEADS, HEAD_DIM), F32),
                        pltpu.SemaphoreType.DMA((PAGE_BUFS,)),
                        pltpu.VMEM((n_blocks, N_HEADS, HEAD_DIM), F32),
                        gathered, gathered,
                        pltpu.SemaphoreType.DMA((2,)),
                        pltpu.SemaphoreType.DMA((2,))])
    return pl.pallas_call(
        _sample_attn_kernel,
        grid_spec=grid_spec,
        out_shape=jax.ShapeDtypeStruct((n_seq, N_HEADS, HEAD_DIM), F32),
        compiler_params=pltpu.CompilerParams(dimension_semantics=("arbitrary",),
                                             vmem_limit_bytes=VMEM_LIMIT),
        name="attn_sample",
    )(page_table, q, k_new, v_new, cache_k, cache_v)


def kernel(x_prompt, x_sample, cache_k, cache_v, state_h, state_conv, page_table, w_in, conv_w, conv_b,
           w_rg_a, b_rg_a, w_rg_x, b_rg_x, lru_lambda, w_br_rnn, w_br_attn, w_out, ln1_g, ln1_b,
           w_up, b_up, w_down, b_down, ln2_g, ln2_b):
    assert w_in.shape[0] == 1, "single-layer trunk"
    b, t, _ = x_prompt.shape
    db = x_sample.shape[0]
    assert x_sample.shape[1] == 1 and t % MOBA_BLOCK == 0

    w_in_b = w_in[0].astype(BF16)
    w_gate = jnp.concatenate([w_rg_a[0], w_rg_x[0]], axis=-1).astype(BF16)
    cw, cb = conv_w[0], conv_b
    post_w = (w_br_rnn[0].astype(BF16), w_br_attn[0].astype(BF16), w_out[0].astype(BF16),
              w_up[0].astype(BF16), w_down[0].astype(BF16), ln1_g, ln1_b, b_up, b_down, ln2_g, ln2_b)

    (rnn_p, qt_p, k_p, v_p, kb_p, vt_p, sga_p, sgb_p, selt_p, h_p, cs_p) = _front_prompt(
        x_prompt, w_in_b, cw, cb, w_gate, b_rg_a, b_rg_x, lru_lambda)
    o_p = _attn_prompt(qt_p, kb_p, vt_p, selt_p)
    flat = lambda a: a.reshape(b * t, D_MODEL)
    y_p = _post(flat(rnn_p), flat(o_p), flat(sga_p), flat(sgb_p), flat(x_prompt), *post_w, tm=256)

    xs = x_sample.reshape(db, D_MODEL)
    sc = state_conv[0]
    (rnn_s, q_s, k_s, v_s, sga_s, sgb_s, h_s, u_s) = _front_sample(
        xs, sc[:, 0], sc[:, 1], sc[:, 2], state_h[0], w_in_b, cw, cb, w_gate, b_rg_a, b_rg_x, lru_lambda)
    heads = lambda a: a.reshape(db, N_HEADS, HEAD_DIM)
    o_s = _sample_attn(page_table, heads(q_s), heads(k_s), heads(v_s), cache_k, cache_v)
    y_s = _post(rnn_s, o_s.reshape(db, D_MODEL).astype(BF16), sga_s, sgb_s, xs, *post_w, tm=db)
    cs_s = jnp.concatenate([sc[:, 1:], u_s[:, None, :]], axis=1)

    kv_p = lambda a: a.reshape(1, b, t, N_HEADS, HEAD_DIM)
    kv_s = lambda a: a.reshape(1, db, 1, N_HEADS, HEAD_DIM)
    return (y_p.reshape(b, t, D_MODEL), y_s.reshape(db, 1, D_MODEL), kv_p(k_p), kv_p(v_p),
            h_p.reshape(1, b, D_MODEL), cs_p[None],
            kv_s(k_s), kv_s(v_s), h_s[None], cs_s[None])
```

```python
import math

import jax
import jax.numpy as jnp
from jax import lax
from jax.experimental import pallas as pl
from jax.experimental.pallas import tpu as pltpu

F32 = jnp.float32
BF16 = jnp.bfloat16

D_MODEL = 1024
N_HEADS = 8
HEAD_DIM = 128
RNN_BLOCKS = 8
RNN_BW = 128
CONV_W = 4
LRU_C = 8.0
MOBA_BLOCK = 256
MOBA_TOPK = 3
PAGE_SIZE = 128
PAGES_PER_BLOCK = MOBA_BLOCK // PAGE_SIZE
D_FF = 4096
ALPHA = 2.0 ** 0.25
LN_EPS = 1e-5
NEG = -1e30
SCALE = HEAD_DIM ** -0.5
LOG2E = math.log2(math.e)

SUBLANES = 8
VMEM_LIMIT = 56 * 1024 * 1024
PAGE_RING = 16
HEADS_PER_ITER = 2


def _sigmoid(x):
    return 0.5 * jnp.tanh(0.5 * x) + 0.5


def _softplus(x):
    return jnp.maximum(x, 0.0) + jnp.log1p(jnp.exp(-jnp.abs(x)))


def _gelu_tanh(x):
    c = math.sqrt(2.0 / math.pi)
    return x * (0.5 * (1.0 + jnp.tanh(c * (x + 0.044715 * (x * x * x)))))


def _layer_norm(x, g, b):
    mu = jnp.mean(x, axis=-1, keepdims=True)
    xc = x - mu
    var = jnp.mean(xc * xc, axis=-1, keepdims=True)
    return xc * lax.rsqrt(var + LN_EPS) * g + b


def _split_bf16(x):
    hi = x.astype(BF16)
    lo = (x - hi.astype(F32)).astype(BF16)
    return hi, lo


def _rglru_block(uc_blk, ucb_blk, wg, ba, bx, neg_c_sp):
    gz = jnp.dot(ucb_blk, wg, preferred_element_type=F32)
    r = _sigmoid(gz[:, :RNN_BW] + ba)
    ig = _sigmoid(gz[:, RNN_BW:] + bx)
    log_a = (LRU_C * r) * neg_c_sp
    a = jnp.exp(log_a)
    t = jnp.tanh(log_a)
    mult = jnp.sqrt(-2.0 * t / (1.0 - t))
    return a, mult * (ig * uc_blk)


def _front_kernel(x_ref, w_ref, cw_ref, cb_ref, wg_ref, ba_ref, bx_ref, lam_ref,
                  rnn_ref, qt_ref, k_ref, v_ref, kb_ref, vt_ref, sga_ref, sgb_ref, selt_ref,
                  h_ref, cs_ref,
                  ubuf, abuf, sbuf, hbuf, hcar, kmt):
    i = pl.program_id(1)
    tt = x_ref.shape[0]
    n_blk = kmt.shape[0] // N_HEADS

    @pl.when(i == 0)
    def _():
        ubuf[0:SUBLANES, :] = jnp.zeros((SUBLANES, D_MODEL), F32)
        hcar[...] = jnp.zeros(hcar.shape, F32)
        kmt[...] = jnp.zeros(kmt.shape, F32)

    xb = x_ref[...].astype(BF16)

    def proj(j):
        return jnp.dot(xb, w_ref[:, j * D_MODEL:(j + 1) * D_MODEL], preferred_element_type=F32)

    u = proj(0)
    ubuf[SUBLANES:SUBLANES + tt, :] = u
    uc = cb_ref[...] + cw_ref[CONV_W - 1:CONV_W, :] * u
    for j in range(CONV_W - 1):
        s = SUBLANES - (CONV_W - 1) + j
        uc = uc + cw_ref[j:j + 1, :] * ubuf[s:s + tt, :]
    cs_ref[...] = ubuf[SUBLANES + tt - (CONV_W - 1):SUBLANES + tt, :]
    ubuf[0:SUBLANES, :] = ubuf[tt:tt + SUBLANES, :]

    neg_c_sp = -_softplus(-lam_ref[...])
    ucb = uc.astype(BF16)
    for n in range(RNN_BLOCKS):
        sl = slice(n * RNN_BW, (n + 1) * RNN_BW)
        a, uin = _rglru_block(uc[:, sl], ucb[:, sl], wg_ref[n], ba_ref[:, sl], bx_ref[:, sl],
                              neg_c_sp[:, sl])
        abuf[:, sl] = a
        sbuf[:, sl] = uin

    sub = lax.broadcasted_iota(jnp.int32, (SUBLANES, D_MODEL), 0)
    h = hcar[0:1, :]
    for r in range(tt // SUBLANES):
        rows = slice(r * SUBLANES, (r + 1) * SUBLANES)
        a = abuf[rows, :]
        s = sbuf[rows, :]
        for sh in (1, 2, 4):
            a_prev = jnp.where(sub >= sh, pltpu.roll(a, sh, 0), 1.0)
            s_prev = jnp.where(sub >= sh, pltpu.roll(s, sh, 0), 0.0)
            s = s + a * s_prev
            a = a * a_prev
        hs_r = s + a * h
        hbuf[rows, :] = hs_r
        h = hs_r[SUBLANES - 1:SUBLANES, :]
    hcar[0:1, :] = h
    h_ref[...] = h

    g = proj(1)
    rnn_ref[...] = (hbuf[...] * _gelu_tanh(g)).astype(BF16)

    qt = proj(2).T
    qt_ref[...] = (qt * (SCALE * LOG2E)).astype(BF16)
    q_hi, q_lo = _split_bf16(qt)
    km_hi, km_lo = _split_bf16(kmt[...])
    nk = kmt.shape[0]
    g2 = jnp.dot(jnp.concatenate([km_hi, km_lo], axis=0), q_hi, preferred_element_type=F32)
    gate = g2[:nk] + g2[nk:] + jnp.dot(km_hi, q_lo, preferred_element_type=F32)
    gn = [gate[n * N_HEADS:(n + 1) * N_HEADS, :] for n in range(n_blk)]
    for n in range(n_blk):
        cnt = jnp.zeros(gn[n].shape, jnp.int32)
        for m in range(n_blk):
            if m == n:
                continue
            beats = (gn[m] >= gn[n]) if m < n else (gn[m] > gn[n])
            cnt = cnt + jnp.where(beats, 1, 0) * jnp.where(m < i, 1, 0)
        keep = jnp.where(cnt < MOBA_TOPK, 1.0, 0.0) * jnp.where(n < i, 1.0, 0.0)
        selt_ref[n * N_HEADS:(n + 1) * N_HEADS, :] = keep.astype(F32)

    kf = proj(3)
    k_ref[...] = kf
    for h in range(N_HEADS):
        kb_ref[h] = kf[:, h * HEAD_DIM:(h + 1) * HEAD_DIM].astype(BF16)
    kmean = jnp.sum(kf, axis=0, keepdims=True) * (1.0 / MOBA_BLOCK)
    head_of_lane = lax.broadcasted_iota(jnp.int32, (N_HEADS, D_MODEL), 1) // HEAD_DIM
    row = lax.broadcasted_iota(jnp.int32, (N_HEADS, D_MODEL), 0)
    kmt[pl.ds(pl.multiple_of(i * N_HEADS, N_HEADS), N_HEADS), :] = jnp.where(
        head_of_lane == row, jnp.broadcast_to(kmean, (N_HEADS, D_MODEL)), 0.0)

    vf = proj(4)
    v_ref[...] = vf
    vt_ref[...] = vf.T.astype(BF16)
    sga_ref[...] = _sigmoid(proj(5))
    sgb_ref[...] = _sigmoid(proj(6))


def _const_spec(shape):
    nd = len(shape)
    return pl.BlockSpec(shape, lambda *_: (0,) * nd, pipeline_mode=pl.Buffered(1))


def _front_prompt(x, w_in, conv_w, conv_b, w_gate, b_a, b_x, lam):
    b, t, _ = x.shape
    tt = MOBA_BLOCK
    nt = t // tt
    tile = pl.BlockSpec((None, tt, D_MODEL), lambda bi, ti: (bi, ti, 0))
    ttile = pl.BlockSpec((None, D_MODEL, tt), lambda bi, ti: (bi, 0, ti))
    big = lambda dt: jax.ShapeDtypeStruct((b, t, D_MODEL), dt)
    tbig = jax.ShapeDtypeStruct((b, D_MODEL, t), BF16)
    hm_tile = pl.BlockSpec((None, N_HEADS, tt, HEAD_DIM), lambda bi, ti: (bi, 0, ti, 0))
    hm_big = jax.ShapeDtypeStruct((b, N_HEADS, t, HEAD_DIM), BF16)
    out_shape = (big(BF16), tbig, big(F32), big(F32), hm_big, tbig, big(F32), big(F32),
                 jax.ShapeDtypeStruct((b, nt, nt * N_HEADS, tt), F32),
                 jax.ShapeDtypeStruct((b, 1, D_MODEL), F32),
                 jax.ShapeDtypeStruct((b, CONV_W - 1, D_MODEL), F32))
    out_specs = (tile, ttile, tile, tile, hm_tile, ttile, tile, tile,
                 pl.BlockSpec((None, None, nt * N_HEADS, tt), lambda bi, ti: (bi, ti, 0, 0)),
                 pl.BlockSpec((None, 1, D_MODEL), lambda bi, ti: (bi, 0, 0)),
                 pl.BlockSpec((None, CONV_W - 1, D_MODEL), lambda bi, ti: (bi, 0, 0)))
    in_specs = [tile, _const_spec(w_in.shape), _const_spec(conv_w.shape), _const_spec(conv_b.shape),
                _const_spec(w_gate.shape), _const_spec(b_a.shape), _const_spec(b_x.shape), _const_spec(lam.shape)]
    return pl.pallas_call(
        _front_kernel,
        grid=(b, nt),
        in_specs=in_specs,
        out_specs=out_specs,
        out_shape=out_shape,
        scratch_shapes=[pltpu.VMEM((SUBLANES + tt, D_MODEL), F32),
                        pltpu.VMEM((tt, D_MODEL), F32),
                        pltpu.VMEM((tt, D_MODEL), F32),
                        pltpu.VMEM((tt, D_MODEL), F32),
                        pltpu.VMEM((SUBLANES, D_MODEL), F32),
                        pltpu.VMEM((nt * N_HEADS, D_MODEL), F32)],
        compiler_params=pltpu.CompilerParams(dimension_semantics=("arbitrary", "arbitrary"),
                                             vmem_limit_bytes=VMEM_LIMIT),
        name="front_prompt",
    )(x, w_in, conv_w, conv_b, w_gate, b_a, b_x, lam)


def _attn_kernel(qt_ref, kb_ref, vt_ref, selt_ref, o_ref):
    i = pl.program_id(1)
    tq = qt_ref.shape[1]
    n_q = kb_ref.shape[1] // MOBA_BLOCK
    key_idx = lax.broadcasted_iota(jnp.int32, (MOBA_BLOCK, tq), 0)
    qry_idx = lax.broadcasted_iota(jnp.int32, (MOBA_BLOCK, tq), 1)
    causal = key_idx <= qry_idx

    def one_head(h, n_past):
        nk = (n_past + 1) * MOBA_BLOCK
        rows = pl.ds(pl.multiple_of(h * HEAD_DIM, HEAD_DIM), HEAD_DIM)
        s = jnp.dot(kb_ref[h, 0:nk, :], qt_ref[rows, :], preferred_element_type=F32)
        parts = []
        for j in range(n_past):
            keep = selt_ref[pl.ds(j * N_HEADS + h, 1), :]
            parts.append(jnp.where(keep > 0.5, s[j * MOBA_BLOCK:(j + 1) * MOBA_BLOCK, :], NEG))
        parts.append(jnp.where(causal, s[n_past * MOBA_BLOCK:, :], NEG))
        s = jnp.concatenate(parts, axis=0) if n_past else parts[0]
        m = jnp.max(s, axis=0, keepdims=True)
        p = jnp.exp2(s - m)
        l = jnp.sum(p, axis=0, keepdims=True)
        ot = jnp.dot(vt_ref[rows, 0:nk], p.astype(BF16), preferred_element_type=F32) * (1.0 / l)
        o_ref[h] = ot.T.astype(BF16)

    def attend(n_past):
        def head_group(g, carry):
            for u in range(HEADS_PER_ITER):
                one_head(g * HEADS_PER_ITER + u, n_past)
            return carry
        lax.fori_loop(0, N_HEADS // HEADS_PER_ITER, head_group, 0)

    for v in range(n_q):
        @pl.when(i == v)
        def _(v=v):
            attend(v)


def _attn_prompt(qt, kb, vt, selt):
    b, _, t = qt.shape
    tq = MOBA_BLOCK
    nq = t // tq
    return pl.pallas_call(
        _attn_kernel,
        grid=(b, nq),
        in_specs=[pl.BlockSpec((None, D_MODEL, tq), lambda bi, qi: (bi, 0, qi)),
                  pl.BlockSpec((None, N_HEADS, t, HEAD_DIM), lambda bi, qi: (bi, 0, 0, 0)),
                  pl.BlockSpec((None, D_MODEL, t), lambda bi, qi: (bi, 0, 0)),
                  pl.BlockSpec((None, None, nq * N_HEADS, tq), lambda bi, qi: (bi, qi, 0, 0))],
        out_specs=pl.BlockSpec((None, N_HEADS, tq, HEAD_DIM), lambda bi, qi: (bi, 0, qi, 0)),
        out_shape=jax.ShapeDtypeStruct((b, N_HEADS, t, HEAD_DIM), BF16),
        compiler_params=pltpu.CompilerParams(dimension_semantics=("arbitrary", "arbitrary"),
                                             vmem_limit_bytes=VMEM_LIMIT),
        name="attn_prompt",
    )(qt, kb, vt, selt)


N_SIDE_SLOTS = 4


def _post_body(rnn_ref, o_ref, sga_ref, sgb_ref, x_ref, wbr_ref, wba_ref, wout_ref, wup_ref, wdn_ref,
               g1_ref, b1_ref, bup_ref, bdn_ref, g2_ref, b2_ref, y_ref, side_work):
    y_rnn = jnp.dot(rnn_ref[...], wbr_ref[...], preferred_element_type=F32)
    o = jnp.concatenate([o_ref[h] for h in range(N_HEADS)], axis=-1)
    y_att = jnp.dot(o, wba_ref[...], preferred_element_type=F32)
    mix = sga_ref[...] * y_rnn + sgb_ref[...] * y_att
    side_work(0)
    t1 = ALPHA * x_ref[...] + jnp.dot(mix.astype(BF16), wout_ref[...], preferred_element_type=F32)
    x1 = _layer_norm(t1, g1_ref[...], b1_ref[...])
    x1b = x1.astype(BF16)
    acc = ALPHA * x1 + bdn_ref[...]
    side_work(1)
    n_chunks = D_FF // D_MODEL
    for c in range(n_chunks):
        cs = slice(c * D_MODEL, (c + 1) * D_MODEL)
        hid = jnp.maximum(jnp.dot(x1b, wup_ref[:, cs], preferred_element_type=F32) + bup_ref[:, cs], 0.0)
        acc = acc + jnp.dot((hid * hid).astype(BF16), wdn_ref[cs, :], preferred_element_type=F32)
        if c == n_chunks // 2 - 1:
            side_work(2)
    side_work(3)
    y_ref[...] = _layer_norm(acc, g2_ref[...], b2_ref[...])


def _post_kernel(*refs):
    _post_body(*refs, side_work=lambda slot: None)


def _page_sum(page):
    rows = page.shape[0]
    parts = jnp.sum(page.reshape(SUBLANES, rows // SUBLANES, N_HEADS, HEAD_DIM), axis=1)
    return jnp.sum(parts, axis=0)


def _post_stream_kernel(pt_ref, rnn_ref, o_ref, sga_ref, sgb_ref, x_ref, ck_ref, *rest):
    consts, (y_ref, ksum_ref, ring, sems) = rest[:11], rest[11:]
    r = pl.program_id(0)
    n_pages = pt_ref.shape[1]
    total = pt_ref.shape[0] * n_pages
    per_step = ksum_ref.shape[0] * PAGES_PER_BLOCK
    per_slot = per_step // N_SIDE_SLOTS
    n_ring = ring.shape[0]
    first = r * per_step

    def page_copy(g, slot):
        return pltpu.make_async_copy(ck_ref.at[0, pt_ref[g // n_pages, lax.rem(g, n_pages)]],
                                     ring.at[slot], sems.at[slot])

    @pl.when(r == 0)
    def _():
        for s in range(n_ring):
            page_copy(s, s).start()

    def side_work(slot_idx):
        for blk in range(slot_idx * per_slot // PAGES_PER_BLOCK, (slot_idx + 1) * per_slot // PAGES_PER_BLOCK):
            acc = jnp.zeros((N_HEADS, HEAD_DIM), F32)
            for t in range(PAGES_PER_BLOCK):
                p = blk * PAGES_PER_BLOCK + t
                slot = p % n_ring
                page_copy(first + p, slot).wait()
                acc = acc + _page_sum(ring[slot])
                nxt = first + p + n_ring

                @pl.when(nxt < total)
                def _():
                    page_copy(nxt, slot).start()
            ksum_ref[blk] = acc

    _post_body(rnn_ref, o_ref, sga_ref, sgb_ref, x_ref, *consts, y_ref, side_work=side_work)


def _post(rnn, o, sga, sgb, x, wbr, wba, wout, wup, wdn, g1, b1, bup, bdn, g2, b2, tm, stream=None):
    m = x.shape[0]
    n_steps = m // tm
    tiles_per_seq = o.shape[2] // tm
    assert o.shape[0] * o.shape[2] == m
    tile = pl.BlockSpec((tm, D_MODEL), lambda r, *_: (r, 0))
    o_tile = pl.BlockSpec((None, N_HEADS, tm, HEAD_DIM),
                          lambda r, *_: (r // tiles_per_seq, 0, r % tiles_per_seq, 0))
    consts = (wbr, wba, wout, wup, wdn, g1, b1, bup, bdn, g2, b2)
    params = pltpu.CompilerParams(dimension_semantics=("arbitrary",), vmem_limit_bytes=VMEM_LIMIT)
    y_shape = jax.ShapeDtypeStruct((m, D_MODEL), F32)
    if stream is None:
        return pl.pallas_call(
            _post_kernel,
            grid=(n_steps,),
            in_specs=[tile, o_tile, tile, tile, tile] + [_const_spec(c.shape) for c in consts],
            out_specs=tile,
            out_shape=y_shape,
            compiler_params=params,
            name="post_m%d" % m,
        )(rnn, o, sga, sgb, x, *consts)
    page_table, cache_k = stream
    n_blocks = page_table.size // PAGES_PER_BLOCK
    blocks_per_step = n_blocks // n_steps
    assert blocks_per_step * n_steps == n_blocks
    assert (blocks_per_step * PAGES_PER_BLOCK) % (N_SIDE_SLOTS * PAGE_RING) == 0
    grid_spec = pltpu.PrefetchScalarGridSpec(
        num_scalar_prefetch=1,
        grid=(n_steps,),
        in_specs=([tile, o_tile, tile, tile, tile, pl.BlockSpec(memory_space=pl.ANY)]
                  + [_const_spec(c.shape) for c in consts]),
        out_specs=(tile, pl.BlockSpec((blocks_per_step, N_HEADS, HEAD_DIM), lambda r, *_: (r, 0, 0))),
        scratch_shapes=[pltpu.VMEM((PAGE_RING, PAGE_SIZE, N_HEADS, HEAD_DIM), F32),
                        pltpu.SemaphoreType.DMA((PAGE_RING,))])
    return pl.pallas_call(
        _post_stream_kernel,
        grid_spec=grid_spec,
        out_shape=(y_shape, jax.ShapeDtypeStruct((n_blocks, N_HEADS, HEAD_DIM), F32)),
        compiler_params=params,
        name="post_stream_m%d" % m,
    )(page_table, rnn, o, sga, sgb, x, cache_k, *consts)


def _front_sample_kernel(x_ref, p0_ref, p1_ref, p2_ref, h0_ref, w_ref, cw_ref, cb_ref, wg_ref, ba_ref,
                         bx_ref, lam_ref,
                         rnn_ref, q_ref, k_ref, v_ref, sga_ref, sgb_ref, h_ref, u_ref):
    xb = x_ref[...].astype(BF16)

    def proj(j):
        return jnp.dot(xb, w_ref[:, j * D_MODEL:(j + 1) * D_MODEL], preferred_element_type=F32)

    u = proj(0)
    u_ref[...] = u
    uc = (cb_ref[...] + cw_ref[0:1, :] * p0_ref[...] + cw_ref[1:2, :] * p1_ref[...]
          + cw_ref[2:3, :] * p2_ref[...] + cw_ref[3:4, :] * u)
    neg_c_sp = -_softplus(-lam_ref[...])
    ucb = uc.astype(BF16)
    g = proj(1)
    for n in range(RNN_BLOCKS):
        sl = slice(n * RNN_BW, (n + 1) * RNN_BW)
        a, uin = _rglru_block(uc[:, sl], ucb[:, sl], wg_ref[n], ba_ref[:, sl], bx_ref[:, sl],
                              neg_c_sp[:, sl])
        h = a * h0_ref[:, sl] + uin
        h_ref[:, sl] = h
        rnn_ref[:, sl] = (h * _gelu_tanh(g[:, sl])).astype(BF16)
    q_ref[...] = proj(2)
    k_ref[...] = proj(3)
    v_ref[...] = proj(4)
    sga_ref[...] = _sigmoid(proj(5))
    sgb_ref[...] = _sigmoid(proj(6))


def _front_sample(x, p0, p1, p2, h0, w_in, conv_w, conv_b, w_gate, b_a, b_x, lam):
    m = x.shape[0]
    args = (x, p0, p1, p2, h0, w_in, conv_w, conv_b, w_gate, b_a, b_x, lam)
    row = lambda dt: jax.ShapeDtypeStruct((m, D_MODEL), dt)
    return pl.pallas_call(
        _front_sample_kernel,
        grid=(1,),
        in_specs=[_const_spec(a.shape) for a in args],
        out_specs=tuple(pl.BlockSpec((m, D_MODEL), lambda r: (0, 0)) for _ in range(8)),
        out_shape=(row(BF16), row(F32), row(F32), row(F32), row(F32), row(F32), row(F32), row(F32)),
        compiler_params=pltpu.CompilerParams(dimension_semantics=("arbitrary",),
                                             vmem_limit_bytes=VMEM_LIMIT),
        name="front_sample",
    )(*args)


def _sample_attn_kernel(pt_ref, ksum_ref, q_ref, kn_ref, vn_ref, ck_ref, cv_ref, o_ref,
                        kbuf, vbuf, ksem, vsem):
    b = pl.program_id(0)
    n_seq = pl.num_programs(0)
    n_blocks = ksum_ref.shape[0]
    n_gather = MOBA_TOPK * PAGES_PER_BLOCK
    par = lax.rem(b, 2)

    gate = jnp.sum(ksum_ref[...] * (1.0 / MOBA_BLOCK) * q_ref[b][None], axis=-1, keepdims=True)
    blk = lax.broadcasted_iota(jnp.int32, gate.shape, 0)

    def gather_copy(src_ref, dst_ref, sem, page, h, slot, buf):
        return pltpu.make_async_copy(src_ref.at[0, page, :, h, :], dst_ref.at[buf, h, slot], sem.at[buf])

    for t in range(MOBA_TOPK):
        mx = jnp.max(gate, axis=0, keepdims=True)
        idx = jnp.min(jnp.where(gate == mx, blk, n_blocks), axis=0, keepdims=True)
        gate = jnp.where(blk == idx, -jnp.inf, gate)
        for h in range(N_HEADS):
            sel = idx[0, h, 0]
            for pg in range(PAGES_PER_BLOCK):
                page = pt_ref[b, sel * PAGES_PER_BLOCK + pg]
                slot = t * PAGES_PER_BLOCK + pg
                gather_copy(ck_ref, kbuf, ksem, page, h, slot, par).start()
                gather_copy(cv_ref, vbuf, vsem, page, h, slot, par).start()

    def attend(seq, buf):
        for h in range(N_HEADS):
            for slot in range(n_gather):
                gather_copy(ck_ref, kbuf, ksem, 0, h, slot, buf).wait()
                gather_copy(cv_ref, vbuf, vsem, 0, h, slot, buf).wait()
        q = q_ref[seq]
        kn = kn_ref[seq]
        vn = vn_ref[seq]
        for h in range(N_HEADS):
            qh = q[h:h + 1, :]
            kh = kbuf[buf, h].reshape(n_gather * PAGE_SIZE, HEAD_DIM)
            vh = vbuf[buf, h].reshape(n_gather * PAGE_SIZE, HEAD_DIM)
            s = jnp.sum(kh * qh, axis=-1, keepdims=True) * SCALE
            s_new = jnp.sum(kn[h:h + 1, :] * qh, axis=-1, keepdims=True) * SCALE
            m = jnp.maximum(jnp.max(s, axis=0, keepdims=True), s_new)
            p = jnp.exp(s - m)
            p_new = jnp.exp(s_new - m)
            l = jnp.sum(p, axis=0, keepdims=True) + p_new
            o = jnp.sum(p * vh, axis=0, keepdims=True) + p_new * vn[h:h + 1, :]
            o_ref[seq, h:h + 1, :] = o / l

    @pl.when(b > 0)
    def _():
        attend(b - 1, 1 - par)

    @pl.when(b == n_seq - 1)
    def _():
        attend(b, par)


def _sample_attn(page_table, ksum, q, k_new, v_new, cache_k, cache_v):
    n_seq, n_pages = page_table.shape
    assert (n_pages * PAGE_SIZE) % MOBA_BLOCK == 0, "own-block cached prefix is not supported"
    n_blocks = n_pages // PAGES_PER_BLOCK
    assert n_blocks >= MOBA_TOPK and ksum.shape[0] == n_seq * n_blocks
    n_gather = MOBA_TOPK * PAGES_PER_BLOCK
    whole = pl.BlockSpec((n_seq, N_HEADS, HEAD_DIM), lambda s, pt: (0, 0, 0))
    anyspec = pl.BlockSpec(memory_space=pl.ANY)
    gathered = pltpu.VMEM((2, N_HEADS, n_gather, PAGE_SIZE, HEAD_DIM), F32)
    grid_spec = pltpu.PrefetchScalarGridSpec(
        num_scalar_prefetch=1,
        grid=(n_seq,),
        in_specs=[pl.BlockSpec((n_blocks, N_HEADS, HEAD_DIM), lambda s, pt: (s, 0, 0)),
                  whole, whole, whole, anyspec, anyspec],
        out_specs=whole,
        scratch_shapes=[gathered, gathered,
                        pltpu.SemaphoreType.DMA((2,)),
                        pltpu.SemaphoreType.DMA((2,))])
    return pl.pallas_call(
        _sample_attn_kernel,
        grid_spec=grid_spec,
        out_shape=jax.ShapeDtypeStruct((n_seq, N_HEADS, HEAD_DIM), F32),
        compiler_params=pltpu.CompilerParams(dimension_semantics=("arbitrary",),
                                             vmem_limit_bytes=VMEM_LIMIT),
        name="attn_sample",
    )(page_table, ksum, q, k_new, v_new, cache_k, cache_v)


def kernel(x_prompt, x_sample, cache_k, cache_v, state_h, state_conv, page_table, w_in, conv_w, conv_b,
           w_rg_a, b_rg_a, w_rg_x, b_rg_x, lru_lambda, w_br_rnn, w_br_attn, w_out, ln1_g, ln1_b,
           w_up, b_up, w_down, b_down, ln2_g, ln2_b):
    assert w_in.shape[0] == 1, "single-layer trunk"
    b, t, _ = x_prompt.shape
    db = x_sample.shape[0]
    assert x_sample.shape[1] == 1 and t % MOBA_BLOCK == 0

    w_in_b = w_in[0].astype(BF16)
    w_gate = jnp.concatenate([w_rg_a[0], w_rg_x[0]], axis=-1).astype(BF16)
    cw, cb = conv_w[0], conv_b
    post_w = (w_br_rnn[0].astype(BF16), w_br_attn[0].astype(BF16), w_out[0].astype(BF16),
              w_up[0].astype(BF16), w_down[0].astype(BF16), ln1_g, ln1_b, b_up, b_down, ln2_g, ln2_b)

    (rnn_p, qt_p, k_p, v_p, kb_p, vt_p, sga_p, sgb_p, selt_p, h_p, cs_p) = _front_prompt(
        x_prompt, w_in_b, cw, cb, w_gate, b_rg_a, b_rg_x, lru_lambda)
    o_p = _attn_prompt(qt_p, kb_p, vt_p, selt_p)
    flat = lambda a: a.reshape(b * t, D_MODEL)
    y_p, ksum = _post(flat(rnn_p), o_p, flat(sga_p), flat(sgb_p), flat(x_prompt), *post_w, tm=256,
                      stream=(page_table, cache_k))

    xs = x_sample.reshape(db, D_MODEL)
    sc = state_conv[0]
    (rnn_s, q_s, k_s, v_s, sga_s, sgb_s, h_s, u_s) = _front_sample(
        xs, sc[:, 0], sc[:, 1], sc[:, 2], state_h[0], w_in_b, cw, cb, w_gate, b_rg_a, b_rg_x, lru_lambda)
    heads = lambda a: a.reshape(db, N_HEADS, HEAD_DIM)
    o_s = _sample_attn(page_table, ksum, heads(q_s), heads(k_s), heads(v_s), cache_k, cache_v)
    o_s_hm = jnp.transpose(o_s, (1, 0, 2)).astype(BF16)[None]
    y_s = _post(rnn_s, o_s_hm, sga_s, sgb_s, xs, *post_w, tm=db)
    cs_s = jnp.concatenate([sc[:, 1:], u_s[:, None, :]], axis=1)

    kv_p = lambda a: a.reshape(1, b, t, N_HEADS, HEAD_DIM)
    kv_s = lambda a: a.reshape(1, db, 1, N_HEADS, HEAD_DIM)
    return (y_p.reshape(b, t, D_MODEL), y_s.reshape(db, 1, D_MODEL), kv_p(k_p), kv_p(v_p),
            h_p.reshape(1, b, D_MODEL), cs_p[None],
            kv_s(k_s), kv_s(v_s), h_s[None], cs_s[None])
```

```python
import functools
import math

import jax
import jax.numpy as jnp
from jax import lax
from jax.experimental import pallas as pl
from jax.experimental.pallas import tpu as pltpu

F32 = jnp.float32
BF16 = jnp.bfloat16

D_MODEL = 1024
N_HEADS = 8
HEAD_DIM = 128
RNN_BLOCKS = 8
RNN_BW = 128
CONV_W = 4
LRU_C = 8.0
MOBA_BLOCK = 256
MOBA_TOPK = 3
PAGE_SIZE = 128
PAGES_PER_BLOCK = MOBA_BLOCK // PAGE_SIZE
D_FF = 4096
ALPHA = 2.0 ** 0.25
LN_EPS = 1e-5
NEG = -1e30
SCALE = HEAD_DIM ** -0.5
LOG2E = math.log2(math.e)

SUBLANES = 8
VMEM_LIMIT = 56 * 1024 * 1024
HEADS_PER_ITER = 2


def _sigmoid(x):
    return 0.5 * jnp.tanh(0.5 * x) + 0.5


def _softplus(x):
    return jnp.maximum(x, 0.0) + jnp.log1p(jnp.exp(-jnp.abs(x)))


def _gelu_tanh(x):
    c = math.sqrt(2.0 / math.pi)
    return x * (0.5 * (1.0 + jnp.tanh(c * (x + 0.044715 * (x * x * x)))))


def _layer_norm(x, g, b):
    mu = jnp.mean(x, axis=-1, keepdims=True)
    xc = x - mu
    var = jnp.mean(xc * xc, axis=-1, keepdims=True)
    return xc * lax.rsqrt(var + LN_EPS) * g + b


def _split_bf16(x):
    hi = x.astype(BF16)
    lo = (x - hi.astype(F32)).astype(BF16)
    return hi, lo


def _rglru_block(uc_blk, ucb_blk, wg, ba, bx, neg_c_sp):
    gz = jnp.dot(ucb_blk, wg, preferred_element_type=F32)
    r = _sigmoid(gz[:, :RNN_BW] + ba)
    ig = _sigmoid(gz[:, RNN_BW:] + bx)
    log_a = (LRU_C * r) * neg_c_sp
    a = jnp.exp(log_a)
    t = jnp.tanh(log_a)
    mult = jnp.sqrt(-2.0 * t / (1.0 - t))
    return a, mult * (ig * uc_blk)


def _page_sum(page):
    rows = page.shape[0]
    parts = jnp.sum(page.reshape(SUBLANES, rows // SUBLANES, N_HEADS, HEAD_DIM), axis=1)
    return jnp.sum(parts, axis=0)


class _PageStream:
    def __init__(self, pt_ref, ck_ref, ring, sems, ksum_ref, step, n_steps, page_base, n_chunks):
        self.pt_ref, self.ck_ref, self.ring, self.sems, self.ksum_ref = pt_ref, ck_ref, ring, sems, ksum_ref
        self.step, self.n_steps, self.page_base, self.n_chunks = step, n_steps, page_base, n_chunks
        self.n_ring = ring.shape[0]
        self.half = self.n_ring // 2
        self.per_step = ksum_ref.shape[0] * PAGES_PER_BLOCK
        assert self.per_step == n_chunks * self.half and n_chunks % 2 == 0
        assert self.half % PAGES_PER_BLOCK == 0
        self.span = self.per_step * n_steps

    def _copy(self, rel, slot):
        n_pages = self.pt_ref.shape[1]
        g = self.page_base + lax.rem(jnp.asarray(rel, jnp.int32), jnp.asarray(self.span, jnp.int32))
        page = self.pt_ref[g // n_pages, lax.rem(g, n_pages)]
        return pltpu.make_async_copy(self.ck_ref.at[0, page], self.ring.at[slot], self.sems.at[slot])

    def _slot(self, c, s):
        return (c % 2) * self.half + s

    def prime(self):
        @pl.when(self.step == 0)
        def _():
            for s in range(self.n_ring):
                self._copy(s, s).start()

    def begin(self, c):
        for s in range(self.half):
            self._copy(0, self._slot(c, s)).wait()
        for blk in range(self.half // PAGES_PER_BLOCK):
            acc = _page_sum(self.ring[self._slot(c, blk * PAGES_PER_BLOCK)])
            for t in range(1, PAGES_PER_BLOCK):
                acc = acc + _page_sum(self.ring[self._slot(c, blk * PAGES_PER_BLOCK + t)])
            self.ksum_ref[c * (self.half // PAGES_PER_BLOCK) + blk] = acc

    def end(self, c):
        for s in range(self.half):
            rel = self.step * self.per_step + c * self.half + s + self.n_ring
            self._copy(rel, self._slot(c, s)).start()

    def drain(self):
        @pl.when(self.step == self.n_steps - 1)
        def _():
            for s in range(self.n_ring):
                self._copy(0, s).wait()


class _NoStream:
    def prime(self): pass
    def begin(self, c): pass
    def end(self, c): pass
    def drain(self): pass


FRONT_STREAM_CHUNKS = 2
FRONT_RING = 16
POST_STREAM_CHUNKS = 4
POST_RING = 24


def _front_kernel(pt_ref, x_ref, ck_ref, w_ref, cw_ref, cb_ref, wg_ref, ba_ref, bx_ref, lam_ref,
                  rnn_ref, qt_ref, k_ref, v_ref, kb_ref, vt_ref, sga_ref, sgb_ref, selt_ref,
                  h_ref, cs_ref, ksum_ref,
                  ubuf, abuf, sbuf, hbuf, hcar, kmt, ring, sems):
    i = pl.program_id(1)
    tt = x_ref.shape[0]
    n_blk = kmt.shape[0] // N_HEADS
    step = pl.program_id(0) * pl.num_programs(1) + i
    stream = _PageStream(pt_ref, ck_ref, ring, sems, ksum_ref, step,
                         pl.num_programs(0) * pl.num_programs(1), 0, FRONT_STREAM_CHUNKS)
    stream.prime()
    stream.begin(0)

    @pl.when(i == 0)
    def _():
        ubuf[0:SUBLANES, :] = jnp.zeros((SUBLANES, D_MODEL), F32)
        hcar[...] = jnp.zeros(hcar.shape, F32)
        kmt[...] = jnp.zeros(kmt.shape, F32)

    xb = x_ref[...].astype(BF16)

    def proj(j):
        return jnp.dot(xb, w_ref[:, j * D_MODEL:(j + 1) * D_MODEL], preferred_element_type=F32)

    u = proj(0)
    ubuf[SUBLANES:SUBLANES + tt, :] = u
    uc = cb_ref[...] + cw_ref[CONV_W - 1:CONV_W, :] * u
    for j in range(CONV_W - 1):
        s = SUBLANES - (CONV_W - 1) + j
        uc = uc + cw_ref[j:j + 1, :] * ubuf[s:s + tt, :]
    cs_ref[...] = ubuf[SUBLANES + tt - (CONV_W - 1):SUBLANES + tt, :]
    ubuf[0:SUBLANES, :] = ubuf[tt:tt + SUBLANES, :]

    neg_c_sp = -_softplus(-lam_ref[...])
    ucb = uc.astype(BF16)
    for n in range(RNN_BLOCKS):
        sl = slice(n * RNN_BW, (n + 1) * RNN_BW)
        a, uin = _rglru_block(uc[:, sl], ucb[:, sl], wg_ref[n], ba_ref[:, sl], bx_ref[:, sl],
                              neg_c_sp[:, sl])
        abuf[:, sl] = a
        sbuf[:, sl] = uin

    sub = lax.broadcasted_iota(jnp.int32, (SUBLANES, D_MODEL), 0)
    h = hcar[0:1, :]
    for r in range(tt // SUBLANES):
        rows = slice(r * SUBLANES, (r + 1) * SUBLANES)
        a = abuf[rows, :]
        s = sbuf[rows, :]
        for sh in (1, 2, 4):
            a_prev = jnp.where(sub >= sh, pltpu.roll(a, sh, 0), 1.0)
            s_prev = jnp.where(sub >= sh, pltpu.roll(s, sh, 0), 0.0)
            s = s + a * s_prev
            a = a * a_prev
        hs_r = s + a * h
        hbuf[rows, :] = hs_r
        h = hs_r[SUBLANES - 1:SUBLANES, :]
    hcar[0:1, :] = h
    h_ref[...] = h
    stream.end(0)
    stream.begin(1)

    g = proj(1)
    rnn_ref[...] = (hbuf[...] * _gelu_tanh(g)).astype(BF16)

    qt = proj(2).T
    qt_ref[...] = (qt * (SCALE * LOG2E)).astype(BF16)
    q_hi, q_lo = _split_bf16(qt)
    km_hi, km_lo = _split_bf16(kmt[...])
    nk = kmt.shape[0]
    g2 = jnp.dot(jnp.concatenate([km_hi, km_lo], axis=0), q_hi, preferred_element_type=F32)
    gate = g2[:nk] + g2[nk:] + jnp.dot(km_hi, q_lo, preferred_element_type=F32)
    gn = [gate[n * N_HEADS:(n + 1) * N_HEADS, :] for n in range(n_blk)]
    for n in range(n_blk):
        cnt = jnp.zeros(gn[n].shape, jnp.int32)
        for m in range(n_blk):
            if m == n:
                continue
            beats = (gn[m] >= gn[n]) if m < n else (gn[m] > gn[n])
            cnt = cnt + jnp.where(beats, 1, 0) * jnp.where(m < i, 1, 0)
        keep = jnp.where(cnt < MOBA_TOPK, 1.0, 0.0) * jnp.where(n < i, 1.0, 0.0)
        selt_ref[n * N_HEADS:(n + 1) * N_HEADS, :] = keep.astype(F32)

    kf = proj(3)
    k_ref[...] = kf
    for h in range(N_HEADS):
        kb_ref[h] = kf[:, h * HEAD_DIM:(h + 1) * HEAD_DIM].astype(BF16)
    kmean = jnp.sum(kf, axis=0, keepdims=True) * (1.0 / MOBA_BLOCK)
    head_of_lane = lax.broadcasted_iota(jnp.int32, (N_HEADS, D_MODEL), 1) // HEAD_DIM
    row = lax.broadcasted_iota(jnp.int32, (N_HEADS, D_MODEL), 0)
    kmt[pl.ds(pl.multiple_of(i * N_HEADS, N_HEADS), N_HEADS), :] = jnp.where(
        head_of_lane == row, jnp.broadcast_to(kmean, (N_HEADS, D_MODEL)), 0.0)

    vf = proj(4)
    v_ref[...] = vf
    vt_ref[...] = vf.T.astype(BF16)
    sga_ref[...] = _sigmoid(proj(5))
    sgb_ref[...] = _sigmoid(proj(6))
    stream.end(1)
    stream.drain()


def _const_spec(shape):
    nd = len(shape)
    return pl.BlockSpec(shape, lambda *_: (0,) * nd, pipeline_mode=pl.Buffered(1))


def _front_prompt(x, w_in, conv_w, conv_b, w_gate, b_a, b_x, lam, page_table, cache_k):
    b, t, _ = x.shape
    tt = MOBA_BLOCK
    nt = t // tt
    blocks_per_step = FRONT_STREAM_CHUNKS * (FRONT_RING // 2) // PAGES_PER_BLOCK
    tile = pl.BlockSpec((None, tt, D_MODEL), lambda bi, ti, *_: (bi, ti, 0))
    ttile = pl.BlockSpec((None, D_MODEL, tt), lambda bi, ti, *_: (bi, 0, ti))
    big = lambda dt: jax.ShapeDtypeStruct((b, t, D_MODEL), dt)
    tbig = jax.ShapeDtypeStruct((b, D_MODEL, t), BF16)
    hm_tile = pl.BlockSpec((None, N_HEADS, tt, HEAD_DIM), lambda bi, ti, *_: (bi, 0, ti, 0))
    hm_big = jax.ShapeDtypeStruct((b, N_HEADS, t, HEAD_DIM), BF16)
    out_shape = (big(BF16), tbig, big(F32), big(F32), hm_big, tbig, big(F32), big(F32),
                 jax.ShapeDtypeStruct((b, nt, nt * N_HEADS, tt), F32),
                 jax.ShapeDtypeStruct((b, 1, D_MODEL), F32),
                 jax.ShapeDtypeStruct((b, CONV_W - 1, D_MODEL), F32),
                 jax.ShapeDtypeStruct((b * nt * blocks_per_step, N_HEADS, HEAD_DIM), F32))
    out_specs = (tile, ttile, tile, tile, hm_tile, ttile, tile, tile,
                 pl.BlockSpec((None, None, nt * N_HEADS, tt), lambda bi, ti, *_: (bi, ti, 0, 0)),
                 pl.BlockSpec((None, 1, D_MODEL), lambda bi, ti, *_: (bi, 0, 0)),
                 pl.BlockSpec((None, CONV_W - 1, D_MODEL), lambda bi, ti, *_: (bi, 0, 0)),
                 pl.BlockSpec((blocks_per_step, N_HEADS, HEAD_DIM), lambda bi, ti, *_: (bi * nt + ti, 0, 0)))
    in_specs = [tile, pl.BlockSpec(memory_space=pl.ANY),
                _const_spec(w_in.shape), _const_spec(conv_w.shape), _const_spec(conv_b.shape),
                _const_spec(w_gate.shape), _const_spec(b_a.shape), _const_spec(b_x.shape), _const_spec(lam.shape)]
    grid_spec = pltpu.PrefetchScalarGridSpec(
        num_scalar_prefetch=1,
        grid=(b, nt),
        in_specs=in_specs,
        out_specs=out_specs,
        scratch_shapes=[pltpu.VMEM((SUBLANES + tt, D_MODEL), F32),
                        pltpu.VMEM((tt, D_MODEL), F32),
                        pltpu.VMEM((tt, D_MODEL), F32),
                        pltpu.VMEM((tt, D_MODEL), F32),
                        pltpu.VMEM((SUBLANES, D_MODEL), F32),
                        pltpu.VMEM((nt * N_HEADS, D_MODEL), F32),
                        pltpu.VMEM((FRONT_RING, PAGE_SIZE, N_HEADS, HEAD_DIM), F32),
                        pltpu.SemaphoreType.DMA((FRONT_RING,))])
    return pl.pallas_call(
        _front_kernel,
        grid_spec=grid_spec,
        out_shape=out_shape,
        compiler_params=pltpu.CompilerParams(dimension_semantics=("arbitrary", "arbitrary"),
                                             vmem_limit_bytes=VMEM_LIMIT),
        name="front_prompt",
    )(page_table, x, cache_k, w_in, conv_w, conv_b, w_gate, b_a, b_x, lam)


def _attn_kernel(qt_ref, kb_ref, vt_ref, selt_ref, o_ref):
    i = pl.program_id(1)
    tq = qt_ref.shape[1]
    n_q = kb_ref.shape[1] // MOBA_BLOCK
    key_idx = lax.broadcasted_iota(jnp.int32, (MOBA_BLOCK, tq), 0)
    qry_idx = lax.broadcasted_iota(jnp.int32, (MOBA_BLOCK, tq), 1)
    causal = key_idx <= qry_idx

    def one_head(h, n_past):
        nk = (n_past + 1) * MOBA_BLOCK
        rows = pl.ds(pl.multiple_of(h * HEAD_DIM, HEAD_DIM), HEAD_DIM)
        s = jnp.dot(kb_ref[h, 0:nk, :], qt_ref[rows, :], preferred_element_type=F32)
        parts = []
        for j in range(n_past):
            keep = selt_ref[pl.ds(j * N_HEADS + h, 1), :]
            parts.append(jnp.where(keep > 0.5, s[j * MOBA_BLOCK:(j + 1) * MOBA_BLOCK, :], NEG))
        parts.append(jnp.where(causal, s[n_past * MOBA_BLOCK:, :], NEG))
        s = jnp.concatenate(parts, axis=0) if n_past else parts[0]
        m = jnp.max(s, axis=0, keepdims=True)
        p = jnp.exp2(s - m)
        l = jnp.sum(p, axis=0, keepdims=True)
        ot = jnp.dot(vt_ref[rows, 0:nk], p.astype(BF16), preferred_element_type=F32) * (1.0 / l)
        o_ref[h] = ot.T.astype(BF16)

    def attend(n_past):
        def head_group(g, carry):
            for u in range(HEADS_PER_ITER):
                one_head(g * HEADS_PER_ITER + u, n_past)
            return carry
        lax.fori_loop(0, N_HEADS // HEADS_PER_ITER, head_group, 0)

    for v in range(n_q):
        @pl.when(i == v)
        def _(v=v):
            attend(v)


def _attn_prompt(qt, kb, vt, selt):
    b, _, t = qt.shape
    tq = MOBA_BLOCK
    nq = t // tq
    return pl.pallas_call(
        _attn_kernel,
        grid=(b, nq),
        in_specs=[pl.BlockSpec((None, D_MODEL, tq), lambda bi, qi: (bi, 0, qi)),
                  pl.BlockSpec((None, N_HEADS, t, HEAD_DIM), lambda bi, qi: (bi, 0, 0, 0)),
                  pl.BlockSpec((None, D_MODEL, t), lambda bi, qi: (bi, 0, 0)),
                  pl.BlockSpec((None, None, nq * N_HEADS, tq), lambda bi, qi: (bi, qi, 0, 0))],
        out_specs=pl.BlockSpec((None, N_HEADS, tq, HEAD_DIM), lambda bi, qi: (bi, 0, qi, 0)),
        out_shape=jax.ShapeDtypeStruct((b, N_HEADS, t, HEAD_DIM), BF16),
        compiler_params=pltpu.CompilerParams(dimension_semantics=("arbitrary", "arbitrary"),
                                             vmem_limit_bytes=VMEM_LIMIT),
        name="attn_prompt",
    )(qt, kb, vt, selt)


def _post_body(rnn_ref, o_ref, sga_ref, sgb_ref, x_ref, wbr_ref, wba_ref, wout_ref, wup_ref, wdn_ref,
               g1_ref, b1_ref, bup_ref, bdn_ref, g2_ref, b2_ref, y_ref, stream):
    stream.begin(0)
    y_rnn = jnp.dot(rnn_ref[...], wbr_ref[...], preferred_element_type=F32)
    o = jnp.concatenate([o_ref[h] for h in range(N_HEADS)], axis=-1)
    y_att = jnp.dot(o, wba_ref[...], preferred_element_type=F32)
    mix = sga_ref[...] * y_rnn + sgb_ref[...] * y_att
    stream.end(0)
    stream.begin(1)
    t1 = ALPHA * x_ref[...] + jnp.dot(mix.astype(BF16), wout_ref[...], preferred_element_type=F32)
    x1 = _layer_norm(t1, g1_ref[...], b1_ref[...])
    x1b = x1.astype(BF16)
    acc = ALPHA * x1 + bdn_ref[...]
    stream.end(1)
    stream.begin(2)
    n_chunks = D_FF // D_MODEL
    for c in range(n_chunks):
        cs = slice(c * D_MODEL, (c + 1) * D_MODEL)
        hid = jnp.maximum(jnp.dot(x1b, wup_ref[:, cs], preferred_element_type=F32) + bup_ref[:, cs], 0.0)
        acc = acc + jnp.dot((hid * hid).astype(BF16), wdn_ref[cs, :], preferred_element_type=F32)
        if c == n_chunks // 2 - 1:
            stream.end(2)
            stream.begin(3)
    y_ref[...] = _layer_norm(acc, g2_ref[...], b2_ref[...])
    stream.end(3)


def _post_kernel(*refs):
    _post_body(*refs, stream=_NoStream())


def _post_stream_kernel(page_base, pt_ref, rnn_ref, o_ref, sga_ref, sgb_ref, x_ref, ck_ref, *rest):
    consts, (y_ref, ksum_ref, ring, sems) = rest[:11], rest[11:]
    stream = _PageStream(pt_ref, ck_ref, ring, sems, ksum_ref, pl.program_id(0), pl.num_programs(0),
                         page_base, POST_STREAM_CHUNKS)
    stream.prime()
    _post_body(rnn_ref, o_ref, sga_ref, sgb_ref, x_ref, *consts, y_ref, stream=stream)
    stream.drain()


def _post(rnn, o, sga, sgb, x, wbr, wba, wout, wup, wdn, g1, b1, bup, bdn, g2, b2, tm, stream=None):
    m = x.shape[0]
    n_steps = m // tm
    tiles_per_seq = o.shape[2] // tm
    assert o.shape[0] * o.shape[2] == m
    tile = pl.BlockSpec((tm, D_MODEL), lambda r, *_: (r, 0))
    o_tile = pl.BlockSpec((None, N_HEADS, tm, HEAD_DIM),
                          lambda r, *_: (r // tiles_per_seq, 0, r % tiles_per_seq, 0))
    consts = (wbr, wba, wout, wup, wdn, g1, b1, bup, bdn, g2, b2)
    params = pltpu.CompilerParams(dimension_semantics=("arbitrary",), vmem_limit_bytes=VMEM_LIMIT)
    y_shape = jax.ShapeDtypeStruct((m, D_MODEL), F32)
    if stream is None:
        return pl.pallas_call(
            _post_kernel,
            grid=(n_steps,),
            in_specs=[tile, o_tile, tile, tile, tile] + [_const_spec(c.shape) for c in consts],
            out_specs=tile,
            out_shape=y_shape,
            compiler_params=params,
            name="post_m%d" % m,
        )(rnn, o, sga, sgb, x, *consts)
    page_table, cache_k, page_base = stream
    blocks_per_step = POST_STREAM_CHUNKS * (POST_RING // 2) // PAGES_PER_BLOCK
    n_blocks = blocks_per_step * n_steps
    assert page_base + n_blocks * PAGES_PER_BLOCK == page_table.size, "front + post shares must cover all pages"
    grid_spec = pltpu.PrefetchScalarGridSpec(
        num_scalar_prefetch=1,
        grid=(n_steps,),
        in_specs=([tile, o_tile, tile, tile, tile, pl.BlockSpec(memory_space=pl.ANY)]
                  + [_const_spec(c.shape) for c in consts]),
        out_specs=(tile, pl.BlockSpec((blocks_per_step, N_HEADS, HEAD_DIM), lambda r, *_: (r, 0, 0))),
        scratch_shapes=[pltpu.VMEM((POST_RING, PAGE_SIZE, N_HEADS, HEAD_DIM), F32),
                        pltpu.SemaphoreType.DMA((POST_RING,))])
    return pl.pallas_call(
        functools.partial(_post_stream_kernel, page_base),
        grid_spec=grid_spec,
        out_shape=(y_shape, jax.ShapeDtypeStruct((n_blocks, N_HEADS, HEAD_DIM), F32)),
        compiler_params=params,
        name="post_stream_m%d" % m,
    )(page_table, rnn, o, sga, sgb, x, cache_k, *consts)


def _front_sample_kernel(x_ref, p0_ref, p1_ref, p2_ref, h0_ref, w_ref, cw_ref, cb_ref, wg_ref, ba_ref,
                         bx_ref, lam_ref,
                         rnn_ref, q_ref, k_ref, v_ref, sga_ref, sgb_ref, h_ref, u_ref):
    xb = x_ref[...].astype(BF16)

    def proj(j):
        return jnp.dot(xb, w_ref[:, j * D_MODEL:(j + 1) * D_MODEL], preferred_element_type=F32)

    u = proj(0)
    u_ref[...] = u
    uc = (cb_ref[...] + cw_ref[0:1, :] * p0_ref[...] + cw_ref[1:2, :] * p1_ref[...]
          + cw_ref[2:3, :] * p2_ref[...] + cw_ref[3:4, :] * u)
    neg_c_sp = -_softplus(-lam_ref[...])
    ucb = uc.astype(BF16)
    g = proj(1)
    for n in range(RNN_BLOCKS):
        sl = slice(n * RNN_BW, (n + 1) * RNN_BW)
        a, uin = _rglru_block(uc[:, sl], ucb[:, sl], wg_ref[n], ba_ref[:, sl], bx_ref[:, sl],
                              neg_c_sp[:, sl])
        h = a * h0_ref[:, sl] + uin
        h_ref[:, sl] = h
        rnn_ref[:, sl] = (h * _gelu_tanh(g[:, sl])).astype(BF16)
    q_ref[...] = proj(2)
    k_ref[...] = proj(3)
    v_ref[...] = proj(4)
    sga_ref[...] = _sigmoid(proj(5))
    sgb_ref[...] = _sigmoid(proj(6))


def _front_sample(x, p0, p1, p2, h0, w_in, conv_w, conv_b, w_gate, b_a, b_x, lam):
    m = x.shape[0]
    args = (x, p0, p1, p2, h0, w_in, conv_w, conv_b, w_gate, b_a, b_x, lam)
    row = lambda dt: jax.ShapeDtypeStruct((m, D_MODEL), dt)
    return pl.pallas_call(
        _front_sample_kernel,
        grid=(1,),
        in_specs=[_const_spec(a.shape) for a in args],
        out_specs=tuple(pl.BlockSpec((m, D_MODEL), lambda r: (0, 0)) for _ in range(8)),
        out_shape=(row(BF16), row(F32), row(F32), row(F32), row(F32), row(F32), row(F32), row(F32)),
        compiler_params=pltpu.CompilerParams(dimension_semantics=("arbitrary",),
                                             vmem_limit_bytes=VMEM_LIMIT),
        name="front_sample",
    )(*args)


def _sample_attn_kernel(pt_ref, ksum_ref, q_ref, kn_ref, vn_ref, ck_ref, cv_ref, o_ref,
                        kbuf, vbuf, ksem, vsem):
    b = pl.program_id(0)
    n_seq = pl.num_programs(0)
    n_blocks = ksum_ref.shape[0]
    n_gather = MOBA_TOPK * PAGES_PER_BLOCK
    par = lax.rem(b, 2)

    gate = jnp.sum(ksum_ref[...] * (1.0 / MOBA_BLOCK) * q_ref[b][None], axis=-1, keepdims=True)
    blk = lax.broadcasted_iota(jnp.int32, gate.shape, 0)

    def gather_copy(src_ref, dst_ref, sem, page, h, slot, buf):
        return pltpu.make_async_copy(src_ref.at[0, page, :, h, :], dst_ref.at[buf, h, slot], sem.at[buf])

    for t in range(MOBA_TOPK):
        mx = jnp.max(gate, axis=0, keepdims=True)
        idx = jnp.min(jnp.where(gate == mx, blk, n_blocks), axis=0, keepdims=True)
        gate = jnp.where(blk == idx, -jnp.inf, gate)
        for h in range(N_HEADS):
            sel = idx[0, h, 0]
            for pg in range(PAGES_PER_BLOCK):
                page = pt_ref[b, sel * PAGES_PER_BLOCK + pg]
                slot = t * PAGES_PER_BLOCK + pg
                gather_copy(ck_ref, kbuf, ksem, page, h, slot, par).start()
                gather_copy(cv_ref, vbuf, vsem, page, h, slot, par).start()

    def attend(seq, buf):
        for h in range(N_HEADS):
            for slot in range(n_gather):
                gather_copy(ck_ref, kbuf, ksem, 0, h, slot, buf).wait()
                gather_copy(cv_ref, vbuf, vsem, 0, h, slot, buf).wait()
        q = q_ref[seq]
        kn = kn_ref[seq]
        vn = vn_ref[seq]
        for h in range(N_HEADS):
            qh = q[h:h + 1, :]
            kh = kbuf[buf, h].reshape(n_gather * PAGE_SIZE, HEAD_DIM)
            vh = vbuf[buf, h].reshape(n_gather * PAGE_SIZE, HEAD_DIM)
            s = jnp.sum(kh * qh, axis=-1, keepdims=True) * SCALE
            s_new = jnp.sum(kn[h:h + 1, :] * qh, axis=-1, keepdims=True) * SCALE
            m = jnp.maximum(jnp.max(s, axis=0, keepdims=True), s_new)
            p = jnp.exp(s - m)
            p_new = jnp.exp(s_new - m)
            l = jnp.sum(p, axis=0, keepdims=True) + p_new
            o = jnp.sum(p * vh, axis=0, keepdims=True) + p_new * vn[h:h + 1, :]
            o_ref[seq, h:h + 1, :] = o / l

    @pl.when(b > 0)
    def _():
        attend(b - 1, 1 - par)

    @pl.when(b == n_seq - 1)
    def _():
        attend(b, par)


def _sample_attn(page_table, ksum, q, k_new, v_new, cache_k, cache_v):
    n_seq, n_pages = page_table.shape
    assert (n_pages * PAGE_SIZE) % MOBA_BLOCK == 0, "own-block cached prefix is not supported"
    n_blocks = n_pages // PAGES_PER_BLOCK
    assert n_blocks >= MOBA_TOPK and ksum.shape[0] == n_seq * n_blocks
    n_gather = MOBA_TOPK * PAGES_PER_BLOCK
    whole = pl.BlockSpec((n_seq, N_HEADS, HEAD_DIM), lambda s, pt: (0, 0, 0))
    anyspec = pl.BlockSpec(memory_space=pl.ANY)
    gathered = pltpu.VMEM((2, N_HEADS, n_gather, PAGE_SIZE, HEAD_DIM), F32)
    grid_spec = pltpu.PrefetchScalarGridSpec(
        num_scalar_prefetch=1,
        grid=(n_seq,),
        in_specs=[pl.BlockSpec((n_blocks, N_HEADS, HEAD_DIM), lambda s, pt: (s, 0, 0)),
                  whole, whole, whole, anyspec, anyspec],
        out_specs=whole,
        scratch_shapes=[gathered, gathered,
                        pltpu.SemaphoreType.DMA((2,)),
                        pltpu.SemaphoreType.DMA((2,))])
    return pl.pallas_call(
        _sample_attn_kernel,
        grid_spec=grid_spec,
        out_shape=jax.ShapeDtypeStruct((n_seq, N_HEADS, HEAD_DIM), F32),
        compiler_params=pltpu.CompilerParams(dimension_semantics=("arbitrary",),
                                             vmem_limit_bytes=VMEM_LIMIT),
        name="attn_sample",
    )(page_table, ksum, q, k_new, v_new, cache_k, cache_v)


def kernel(x_prompt, x_sample, cache_k, cache_v, state_h, state_conv, page_table, w_in, conv_w, conv_b,
           w_rg_a, b_rg_a, w_rg_x, b_rg_x, lru_lambda, w_br_rnn, w_br_attn, w_out, ln1_g, ln1_b,
           w_up, b_up, w_down, b_down, ln2_g, ln2_b):
    assert w_in.shape[0] == 1, "single-layer trunk"
    b, t, _ = x_prompt.shape
    db = x_sample.shape[0]
    assert x_sample.shape[1] == 1 and t % MOBA_BLOCK == 0

    w_in_b = w_in[0].astype(BF16)
    w_gate = jnp.concatenate([w_rg_a[0], w_rg_x[0]], axis=-1).astype(BF16)
    cw, cb = conv_w[0], conv_b
    post_w = (w_br_rnn[0].astype(BF16), w_br_attn[0].astype(BF16), w_out[0].astype(BF16),
              w_up[0].astype(BF16), w_down[0].astype(BF16), ln1_g, ln1_b, b_up, b_down, ln2_g, ln2_b)

    (rnn_p, qt_p, k_p, v_p, kb_p, vt_p, sga_p, sgb_p, selt_p, h_p, cs_p, ksum_a) = _front_prompt(
        x_prompt, w_in_b, cw, cb, w_gate, b_rg_a, b_rg_x, lru_lambda, page_table, cache_k)
    o_p = _attn_prompt(qt_p, kb_p, vt_p, selt_p)
    flat = lambda a: a.reshape(b * t, D_MODEL)
    y_p, ksum_b = _post(flat(rnn_p), o_p, flat(sga_p), flat(sgb_p), flat(x_prompt), *post_w, tm=256,
                        stream=(page_table, cache_k, ksum_a.shape[0] * PAGES_PER_BLOCK))
    ksum = jnp.concatenate([ksum_a, ksum_b], axis=0)

    xs = x_sample.reshape(db, D_MODEL)
    sc = state_conv[0]
    (rnn_s, q_s, k_s, v_s, sga_s, sgb_s, h_s, u_s) = _front_sample(
        xs, sc[:, 0], sc[:, 1], sc[:, 2], state_h[0], w_in_b, cw, cb, w_gate, b_rg_a, b_rg_x, lru_lambda)
    heads = lambda a: a.reshape(db, N_HEADS, HEAD_DIM)
    o_s = _sample_attn(page_table, ksum, heads(q_s), heads(k_s), heads(v_s), cache_k, cache_v)
    o_s_hm = jnp.transpose(o_s, (1, 0, 2)).astype(BF16)[None]
    y_s = _post(rnn_s, o_s_hm, sga_s, sgb_s, xs, *post_w, tm=db)
    cs_s = jnp.concatenate([sc[:, 1:], u_s[:, None, :]], axis=1)

    kv_p = lambda a: a.reshape(1, b, t, N_HEADS, HEAD_DIM)
    kv_s = lambda a: a.reshape(1, db, 1, N_HEADS, HEAD_DIM)
    return (y_p.reshape(b, t, D_MODEL), y_s.reshape(db, 1, D_MODEL), kv_p(k_p), kv_p(v_p),
            h_p.reshape(1, b, D_MODEL), cs_p[None],
            kv_s(k_s), kv_s(v_s), h_s[None], cs_s[None])
```

```python
import functools
import math

import jax
import jax.numpy as jnp
from jax import lax
from jax.experimental import pallas as pl
from jax.experimental.pallas import tpu as pltpu

F32 = jnp.float32
BF16 = jnp.bfloat16

D_MODEL = 1024
N_HEADS = 8
HEAD_DIM = 128
RNN_BLOCKS = 8
RNN_BW = 128
CONV_W = 4
LRU_C = 8.0
MOBA_BLOCK = 256
MOBA_TOPK = 3
PAGE_SIZE = 128
PAGES_PER_BLOCK = MOBA_BLOCK // PAGE_SIZE
D_FF = 4096
ALPHA = 2.0 ** 0.25
LN_EPS = 1e-5
NEG = -1e30
SCALE = HEAD_DIM ** -0.5
LOG2E = math.log2(math.e)

SUBLANES = 8
VMEM_LIMIT = 56 * 1024 * 1024
HEADS_PER_ITER = 2


def _sigmoid(x):
    return 0.5 * jnp.tanh(0.5 * x) + 0.5


def _softplus(x):
    return jnp.maximum(x, 0.0) + jnp.log1p(jnp.exp(-jnp.abs(x)))


def _gelu_tanh(x):
    c = math.sqrt(2.0 / math.pi)
    return x * (0.5 * (1.0 + jnp.tanh(c * (x + 0.044715 * (x * x * x)))))


def _layer_norm(x, g, b):
    mu = jnp.mean(x, axis=-1, keepdims=True)
    xc = x - mu
    var = jnp.mean(xc * xc, axis=-1, keepdims=True)
    return xc * lax.rsqrt(var + LN_EPS) * g + b


def _split_bf16(x):
    hi = x.astype(BF16)
    lo = (x - hi.astype(F32)).astype(BF16)
    return hi, lo


def _rglru_block(uc_blk, ucb_blk, wg, ba, bx, neg_c_sp):
    gz = jnp.dot(ucb_blk, wg, preferred_element_type=F32)
    r = _sigmoid(gz[:, :RNN_BW] + ba)
    ig = _sigmoid(gz[:, RNN_BW:] + bx)
    log_a = (LRU_C * r) * neg_c_sp
    a = jnp.exp(log_a)
    t = jnp.tanh(log_a)
    mult = jnp.sqrt(-2.0 * t / (1.0 - t))
    return a, mult * (ig * uc_blk)


def _page_sum(page):
    rows = page.shape[0]
    parts = jnp.sum(page.reshape(SUBLANES, rows // SUBLANES, N_HEADS, HEAD_DIM), axis=1)
    return jnp.sum(parts, axis=0)


class _PageStream:
    def __init__(self, pt_ref, ck_ref, ring, sems, ksum_ref, step, n_steps, page_base, n_chunks):
        self.pt_ref, self.ck_ref, self.ring, self.sems, self.ksum_ref = pt_ref, ck_ref, ring, sems, ksum_ref
        self.step, self.n_steps, self.page_base, self.n_chunks = step, n_steps, page_base, n_chunks
        self.n_ring = ring.shape[0]
        self.half = self.n_ring // 2
        self.per_step = ksum_ref.shape[0] * PAGES_PER_BLOCK
        assert self.per_step == n_chunks * self.half and n_chunks % 2 == 0
        assert self.half % PAGES_PER_BLOCK == 0
        self.span = self.per_step * n_steps

    def _copy(self, rel, slot):
        n_pages = self.pt_ref.shape[1]
        g = self.page_base + lax.rem(jnp.asarray(rel, jnp.int32), jnp.asarray(self.span, jnp.int32))
        page = self.pt_ref[g // n_pages, lax.rem(g, n_pages)]
        return pltpu.make_async_copy(self.ck_ref.at[0, page], self.ring.at[slot], self.sems.at[slot])

    def _slot(self, c, s):
        return (c % 2) * self.half + s

    def prime(self):
        @pl.when(self.step == 0)
        def _():
            for s in range(self.n_ring):
                self._copy(s, s).start()

    def begin(self, c):
        for s in range(self.half):
            self._copy(0, self._slot(c, s)).wait()

    def sums(self, c, part, n_parts):
        blocks = self.half // PAGES_PER_BLOCK
        assert blocks % n_parts == 0
        for blk in range(part * blocks // n_parts, (part + 1) * blocks // n_parts):
            acc = _page_sum(self.ring[self._slot(c, blk * PAGES_PER_BLOCK)])
            for t in range(1, PAGES_PER_BLOCK):
                acc = acc + _page_sum(self.ring[self._slot(c, blk * PAGES_PER_BLOCK + t)])
            self.ksum_ref[c * blocks + blk] = acc

    def end(self, c):
        for s in range(self.half):
            rel = self.step * self.per_step + c * self.half + s + self.n_ring
            self._copy(rel, self._slot(c, s)).start()

    def drain(self):
        @pl.when(self.step == self.n_steps - 1)
        def _():
            for s in range(self.n_ring):
                self._copy(0, s).wait()


class _NoStream:
    def prime(self): pass
    def begin(self, c): pass
    def sums(self, c, part, n_parts): pass
    def end(self, c): pass
    def drain(self): pass


STREAM_CHUNKS = 4
FRONT_RING = 16
POST_RING = 16


def _front_kernel(pt_ref, x_ref, ck_ref, w_ref, cw_ref, cb_ref, wg_ref, ba_ref, bx_ref, lam_ref,
                  rnn_ref, qt_ref, k_ref, v_ref, kb_ref, vt_ref, sga_ref, sgb_ref, selt_ref,
                  h_ref, cs_ref, ksum_ref,
                  ubuf, abuf, sbuf, hbuf, hcar, kmt, ring, sems):
    i = pl.program_id(1)
    tt = x_ref.shape[0]
    n_blk = kmt.shape[0] // N_HEADS
    step = pl.program_id(0) * pl.num_programs(1) + i
    stream = _PageStream(pt_ref, ck_ref, ring, sems, ksum_ref, step,
                         pl.num_programs(0) * pl.num_programs(1), 0, STREAM_CHUNKS)
    stream.prime()
    stream.begin(0)

    @pl.when(i == 0)
    def _():
        ubuf[0:SUBLANES, :] = jnp.zeros((SUBLANES, D_MODEL), F32)
        hcar[...] = jnp.zeros(hcar.shape, F32)
        kmt[...] = jnp.zeros(kmt.shape, F32)

    xb = x_ref[...].astype(BF16)

    def proj(j):
        return jnp.dot(xb, w_ref[:, j * D_MODEL:(j + 1) * D_MODEL], preferred_element_type=F32)


    def rglru_blocks(uc, ucb, neg_c_sp, lo, hi):
        for n in range(lo, hi):
            sl = slice(n * RNN_BW, (n + 1) * RNN_BW)
            a, uin = _rglru_block(uc[:, sl], ucb[:, sl], wg_ref[n], ba_ref[:, sl], bx_ref[:, sl],
                                  neg_c_sp[:, sl])
            abuf[:, sl] = a
            sbuf[:, sl] = uin

    def scan_groups(h, lo, hi):
        sub = lax.broadcasted_iota(jnp.int32, (SUBLANES, D_MODEL), 0)
        for r in range(lo, hi):
            rows = slice(r * SUBLANES, (r + 1) * SUBLANES)
            a = abuf[rows, :]
            s = sbuf[rows, :]
            for sh in (1, 2, 4):
                a_prev = jnp.where(sub >= sh, pltpu.roll(a, sh, 0), 1.0)
                s_prev = jnp.where(sub >= sh, pltpu.roll(s, sh, 0), 0.0)
                s = s + a * s_prev
                a = a * a_prev
            hs_r = s + a * h
            hbuf[rows, :] = hs_r
            h = hs_r[SUBLANES - 1:SUBLANES, :]
        return h

    def query_side():
        qt = proj(2).T
        qt_ref[...] = (qt * (SCALE * LOG2E)).astype(BF16)
        q_hi, q_lo = _split_bf16(qt)
        km_hi, km_lo = _split_bf16(kmt[...])
        nk = kmt.shape[0]
        g2 = jnp.dot(jnp.concatenate([km_hi, km_lo], axis=0), q_hi, preferred_element_type=F32)
        gate = g2[:nk] + g2[nk:] + jnp.dot(km_hi, q_lo, preferred_element_type=F32)
        gn = [gate[n * N_HEADS:(n + 1) * N_HEADS, :] for n in range(n_blk)]
        for n in range(n_blk):
            cnt = jnp.zeros(gn[n].shape, jnp.int32)
            for m in range(n_blk):
                if m == n:
                    continue
                beats = (gn[m] >= gn[n]) if m < n else (gn[m] > gn[n])
                cnt = cnt + jnp.where(beats, 1, 0) * jnp.where(m < i, 1, 0)
            keep = jnp.where(cnt < MOBA_TOPK, 1.0, 0.0) * jnp.where(n < i, 1.0, 0.0)
            selt_ref[n * N_HEADS:(n + 1) * N_HEADS, :] = keep.astype(F32)

    def key_side():
        kf = proj(3)
        k_ref[...] = kf
        for hd in range(N_HEADS):
            kb_ref[hd] = kf[:, hd * HEAD_DIM:(hd + 1) * HEAD_DIM].astype(BF16)
        kmean = jnp.sum(kf, axis=0, keepdims=True) * (1.0 / MOBA_BLOCK)
        head_of_lane = lax.broadcasted_iota(jnp.int32, (N_HEADS, D_MODEL), 1) // HEAD_DIM
        row = lax.broadcasted_iota(jnp.int32, (N_HEADS, D_MODEL), 0)
        kmt[pl.ds(pl.multiple_of(i * N_HEADS, N_HEADS), N_HEADS), :] = jnp.where(
            head_of_lane == row, jnp.broadcast_to(kmean, (N_HEADS, D_MODEL)), 0.0)

    u = proj(0)
    ubuf[SUBLANES:SUBLANES + tt, :] = u
    uc = cb_ref[...] + cw_ref[CONV_W - 1:CONV_W, :] * u
    for j in range(CONV_W - 1):
        s = SUBLANES - (CONV_W - 1) + j
        uc = uc + cw_ref[j:j + 1, :] * ubuf[s:s + tt, :]
    cs_ref[...] = ubuf[SUBLANES + tt - (CONV_W - 1):SUBLANES + tt, :]
    ubuf[0:SUBLANES, :] = ubuf[tt:tt + SUBLANES, :]
    neg_c_sp = -_softplus(-lam_ref[...])
    ucb = uc.astype(BF16)
    stream.sums(0, 0, 2)
    rglru_blocks(uc, ucb, neg_c_sp, 0, 2)
    stream.sums(0, 1, 2)
    stream.end(0)

    stream.begin(1)
    query_side()
    stream.sums(1, 0, 2)
    rglru_blocks(uc, ucb, neg_c_sp, 2, 4)
    key_side()
    stream.sums(1, 1, 2)
    stream.end(1)

    stream.begin(2)
    rglru_blocks(uc, ucb, neg_c_sp, 4, 6)
    vf = proj(4)
    v_ref[...] = vf
    vt_ref[...] = vf.T.astype(BF16)
    stream.sums(2, 0, 2)
    rglru_blocks(uc, ucb, neg_c_sp, 6, RNN_BLOCKS)
    sga_ref[...] = _sigmoid(proj(5))
    stream.sums(2, 1, 2)
    stream.end(2)

    stream.begin(3)
    n_groups = tt // SUBLANES
    h = scan_groups(hcar[0:1, :], 0, n_groups // 2)
    stream.sums(3, 0, 2)
    sgb_ref[...] = _sigmoid(proj(6))
    h = scan_groups(h, n_groups // 2, n_groups)
    hcar[0:1, :] = h
    h_ref[...] = h
    stream.sums(3, 1, 2)
    g = proj(1)
    rnn_ref[...] = (hbuf[...] * _gelu_tanh(g)).astype(BF16)
    stream.end(3)
    stream.drain()


def _const_spec(shape):
    nd = len(shape)
    return pl.BlockSpec(shape, lambda *_: (0,) * nd, pipeline_mode=pl.Buffered(1))


def _front_prompt(x, w_in, conv_w, conv_b, w_gate, b_a, b_x, lam, page_table, cache_k):
    b, t, _ = x.shape
    tt = MOBA_BLOCK
    nt = t // tt
    blocks_per_step = STREAM_CHUNKS * (FRONT_RING // 2) // PAGES_PER_BLOCK
    tile = pl.BlockSpec((None, tt, D_MODEL), lambda bi, ti, *_: (bi, ti, 0))
    ttile = pl.BlockSpec((None, D_MODEL, tt), lambda bi, ti, *_: (bi, 0, ti))
    big = lambda dt: jax.ShapeDtypeStruct((b, t, D_MODEL), dt)
    tbig = jax.ShapeDtypeStruct((b, D_MODEL, t), BF16)
    hm_tile = pl.BlockSpec((None, N_HEADS, tt, HEAD_DIM), lambda bi, ti, *_: (bi, 0, ti, 0))
    hm_big = jax.ShapeDtypeStruct((b, N_HEADS, t, HEAD_DIM), BF16)
    out_shape = (big(BF16), tbig, big(F32), big(F32), hm_big, tbig, big(F32), big(F32),
                 jax.ShapeDtypeStruct((b, nt, nt * N_HEADS, tt), F32),
                 jax.ShapeDtypeStruct((b, 1, D_MODEL), F32),
                 jax.ShapeDtypeStruct((b, CONV_W - 1, D_MODEL), F32),
                 jax.ShapeDtypeStruct((b * nt * blocks_per_step, N_HEADS, HEAD_DIM), F32))
    out_specs = (tile, ttile, tile, tile, hm_tile, ttile, tile, tile,
                 pl.BlockSpec((None, None, nt * N_HEADS, tt), lambda bi, ti, *_: (bi, ti, 0, 0)),
                 pl.BlockSpec((None, 1, D_MODEL), lambda bi, ti, *_: (bi, 0, 0)),
                 pl.BlockSpec((None, CONV_W - 1, D_MODEL), lambda bi, ti, *_: (bi, 0, 0)),
                 pl.BlockSpec((blocks_per_step, N_HEADS, HEAD_DIM), lambda bi, ti, *_: (bi * nt + ti, 0, 0)))
    in_specs = [tile, pl.BlockSpec(memory_space=pl.ANY),
                _const_spec(w_in.shape), _const_spec(conv_w.shape), _const_spec(conv_b.shape),
                _const_spec(w_gate.shape), _const_spec(b_a.shape), _const_spec(b_x.shape), _const_spec(lam.shape)]
    grid_spec = pltpu.PrefetchScalarGridSpec(
        num_scalar_prefetch=1,
        grid=(b, nt),
        in_specs=in_specs,
        out_specs=out_specs,
        scratch_shapes=[pltpu.VMEM((SUBLANES + tt, D_MODEL), F32),
                        pltpu.VMEM((tt, D_MODEL), F32),
                        pltpu.VMEM((tt, D_MODEL), F32),
                        pltpu.VMEM((tt, D_MODEL), F32),
                        pltpu.VMEM((SUBLANES, D_MODEL), F32),
                        pltpu.VMEM((nt * N_HEADS, D_MODEL), F32),
                        pltpu.VMEM((FRONT_RING, PAGE_SIZE, N_HEADS, HEAD_DIM), F32),
                        pltpu.SemaphoreType.DMA((FRONT_RING,))])
    return pl.pallas_call(
        _front_kernel,
        grid_spec=grid_spec,
        out_shape=out_shape,
        compiler_params=pltpu.CompilerParams(dimension_semantics=("arbitrary", "arbitrary"),
                                             vmem_limit_bytes=VMEM_LIMIT),
        name="front_prompt",
    )(page_table, x, cache_k, w_in, conv_w, conv_b, w_gate, b_a, b_x, lam)


def _attn_kernel(qt_ref, kb_ref, vt_ref, selt_ref, o_ref):
    i = pl.program_id(1)
    tq = qt_ref.shape[1]
    n_q = kb_ref.shape[1] // MOBA_BLOCK
    key_idx = lax.broadcasted_iota(jnp.int32, (MOBA_BLOCK, tq), 0)
    qry_idx = lax.broadcasted_iota(jnp.int32, (MOBA_BLOCK, tq), 1)
    causal = key_idx <= qry_idx

    def one_head(h, n_past):
        nk = (n_past + 1) * MOBA_BLOCK
        rows = pl.ds(pl.multiple_of(h * HEAD_DIM, HEAD_DIM), HEAD_DIM)
        s = jnp.dot(kb_ref[h, 0:nk, :], qt_ref[rows, :], preferred_element_type=F32)
        parts = []
        for j in range(n_past):
            keep = selt_ref[pl.ds(j * N_HEADS + h, 1), :]
            parts.append(jnp.where(keep > 0.5, s[j * MOBA_BLOCK:(j + 1) * MOBA_BLOCK, :], NEG))
        parts.append(jnp.where(causal, s[n_past * MOBA_BLOCK:, :], NEG))
        s = jnp.concatenate(parts, axis=0) if n_past else parts[0]
        m = jnp.max(s, axis=0, keepdims=True)
        p = jnp.exp2(s - m)
        l = jnp.sum(p, axis=0, keepdims=True)
        ot = jnp.dot(vt_ref[rows, 0:nk], p.astype(BF16), preferred_element_type=F32) * (1.0 / l)
        o_ref[h] = ot.T.astype(BF16)

    def attend(n_past):
        def head_group(g, carry):
            for u in range(HEADS_PER_ITER):
                one_head(g * HEADS_PER_ITER + u, n_past)
            return carry
        lax.fori_loop(0, N_HEADS // HEADS_PER_ITER, head_group, 0)

    for v in range(n_q):
        @pl.when(i == v)
        def _(v=v):
            attend(v)


def _attn_prompt(qt, kb, vt, selt):
    b, _, t = qt.shape
    tq = MOBA_BLOCK
    nq = t // tq
    return pl.pallas_call(
        _attn_kernel,
        grid=(b, nq),
        in_specs=[pl.BlockSpec((None, D_MODEL, tq), lambda bi, qi: (bi, 0, qi)),
                  pl.BlockSpec((None, N_HEADS, t, HEAD_DIM), lambda bi, qi: (bi, 0, 0, 0)),
                  pl.BlockSpec((None, D_MODEL, t), lambda bi, qi: (bi, 0, 0)),
                  pl.BlockSpec((None, None, nq * N_HEADS, tq), lambda bi, qi: (bi, qi, 0, 0))],
        out_specs=pl.BlockSpec((None, N_HEADS, tq, HEAD_DIM), lambda bi, qi: (bi, 0, qi, 0)),
        out_shape=jax.ShapeDtypeStruct((b, N_HEADS, t, HEAD_DIM), BF16),
        compiler_params=pltpu.CompilerParams(dimension_semantics=("arbitrary", "arbitrary"),
                                             vmem_limit_bytes=VMEM_LIMIT),
        name="attn_prompt",
    )(qt, kb, vt, selt)


def _post_body(rnn_ref, o_ref, sga_ref, sgb_ref, x_ref, wbr_ref, wba_ref, wout_ref, wup_ref, wdn_ref,
               g1_ref, b1_ref, bup_ref, bdn_ref, g2_ref, b2_ref, y_ref, stream):
    stream.begin(0)
    y_rnn = jnp.dot(rnn_ref[...], wbr_ref[...], preferred_element_type=F32)
    stream.sums(0, 0, 2)
    o = jnp.concatenate([o_ref[h] for h in range(N_HEADS)], axis=-1)
    y_att = jnp.dot(o, wba_ref[...], preferred_element_type=F32)
    stream.sums(0, 1, 2)
    mix = sga_ref[...] * y_rnn + sgb_ref[...] * y_att
    stream.end(0)
    stream.begin(1)
    t1 = ALPHA * x_ref[...] + jnp.dot(mix.astype(BF16), wout_ref[...], preferred_element_type=F32)
    stream.sums(1, 0, 2)
    x1 = _layer_norm(t1, g1_ref[...], b1_ref[...])
    x1b = x1.astype(BF16)
    acc = ALPHA * x1 + bdn_ref[...]
    stream.sums(1, 1, 2)
    stream.end(1)
    n_chunks = D_FF // D_MODEL
    for c in range(n_chunks):
        stage, part = 2 + c // 2, c % 2
        if part == 0:
            stream.begin(stage)
        cs = slice(c * D_MODEL, (c + 1) * D_MODEL)
        hid = jnp.maximum(jnp.dot(x1b, wup_ref[:, cs], preferred_element_type=F32) + bup_ref[:, cs], 0.0)
        stream.sums(stage, 2 * part, 4)
        acc = acc + jnp.dot((hid * hid).astype(BF16), wdn_ref[cs, :], preferred_element_type=F32)
        stream.sums(stage, 2 * part + 1, 4)
        if part == 1 and stage == 2:
            stream.end(stage)
    y_ref[...] = _layer_norm(acc, g2_ref[...], b2_ref[...])
    stream.end(3)


def _post_kernel(*refs):
    _post_body(*refs, stream=_NoStream())


def _post_stream_kernel(page_base, pt_ref, rnn_ref, o_ref, sga_ref, sgb_ref, x_ref, ck_ref, *rest):
    consts, (y_ref, ksum_ref, ring, sems) = rest[:11], rest[11:]
    stream = _PageStream(pt_ref, ck_ref, ring, sems, ksum_ref, pl.program_id(0), pl.num_programs(0),
                         page_base, STREAM_CHUNKS)
    stream.prime()
    _post_body(rnn_ref, o_ref, sga_ref, sgb_ref, x_ref, *consts, y_ref, stream=stream)
    stream.drain()


def _post(rnn, o, sga, sgb, x, wbr, wba, wout, wup, wdn, g1, b1, bup, bdn, g2, b2, tm, stream=None):
    m = x.shape[0]
    n_steps = m // tm
    tiles_per_seq = o.shape[2] // tm
    assert o.shape[0] * o.shape[2] == m
    tile = pl.BlockSpec((tm, D_MODEL), lambda r, *_: (r, 0))
    o_tile = pl.BlockSpec((None, N_HEADS, tm, HEAD_DIM),
                          lambda r, *_: (r // tiles_per_seq, 0, r % tiles_per_seq, 0))
    consts = (wbr, wba, wout, wup, wdn, g1, b1, bup, bdn, g2, b2)
    params = pltpu.CompilerParams(dimension_semantics=("arbitrary",), vmem_limit_bytes=VMEM_LIMIT)
    y_shape = jax.ShapeDtypeStruct((m, D_MODEL), F32)
    if stream is None:
        return pl.pallas_call(
            _post_kernel,
            grid=(n_steps,),
            in_specs=[tile, o_tile, tile, tile, tile] + [_const_spec(c.shape) for c in consts],
            out_specs=tile,
            out_shape=y_shape,
            compiler_params=params,
            name="post_m%d" % m,
        )(rnn, o, sga, sgb, x, *consts)
    page_table, cache_k, page_base = stream
    blocks_per_step = STREAM_CHUNKS * (POST_RING // 2) // PAGES_PER_BLOCK
    n_blocks = blocks_per_step * n_steps
    assert page_base + n_blocks * PAGES_PER_BLOCK == page_table.size, "front + post shares must cover all pages"
    grid_spec = pltpu.PrefetchScalarGridSpec(
        num_scalar_prefetch=1,
        grid=(n_steps,),
        in_specs=([tile, o_tile, tile, tile, tile, pl.BlockSpec(memory_space=pl.ANY)]
                  + [_const_spec(c.shape) for c in consts]),
        out_specs=(tile, pl.BlockSpec((blocks_per_step, N_HEADS, HEAD_DIM), lambda r, *_: (r, 0, 0))),
        scratch_shapes=[pltpu.VMEM((POST_RING, PAGE_SIZE, N_HEADS, HEAD_DIM), F32),
                        pltpu.SemaphoreType.DMA((POST_RING,))])
    return pl.pallas_call(
        functools.partial(_post_stream_kernel, page_base),
        grid_spec=grid_spec,
        out_shape=(y_shape, jax.ShapeDtypeStruct((n_blocks, N_HEADS, HEAD_DIM), F32)),
        compiler_params=params,
        name="post_stream_m%d" % m,
    )(page_table, rnn, o, sga, sgb, x, cache_k, *consts)


def _front_sample_kernel(x_ref, p0_ref, p1_ref, p2_ref, h0_ref, w_ref, cw_ref, cb_ref, wg_ref, ba_ref,
                         bx_ref, lam_ref,
                         rnn_ref, q_ref, k_ref, v_ref, sga_ref, sgb_ref, h_ref, u_ref):
    xb = x_ref[...].astype(BF16)

    def proj(j):
        return jnp.dot(xb, w_ref[:, j * D_MODEL:(j + 1) * D_MODEL], preferred_element_type=F32)

    u = proj(0)
    u_ref[...] = u
    uc = (cb_ref[...] + cw_ref[0:1, :] * p0_ref[...] + cw_ref[1:2, :] * p1_ref[...]
          + cw_ref[2:3, :] * p2_ref[...] + cw_ref[3:4, :] * u)
    neg_c_sp = -_softplus(-lam_ref[...])
    ucb = uc.astype(BF16)
    g = proj(1)
    for n in range(RNN_BLOCKS):
        sl = slice(n * RNN_BW, (n + 1) * RNN_BW)
        a, uin = _rglru_block(uc[:, sl], ucb[:, sl], wg_ref[n], ba_ref[:, sl], bx_ref[:, sl],
                              neg_c_sp[:, sl])
        h = a * h0_ref[:, sl] + uin
        h_ref[:, sl] = h
        rnn_ref[:, sl] = (h * _gelu_tanh(g[:, sl])).astype(BF16)
    q_ref[...] = proj(2)
    k_ref[...] = proj(3)
    v_ref[...] = proj(4)
    sga_ref[...] = _sigmoid(proj(5))
    sgb_ref[...] = _sigmoid(proj(6))


def _front_sample(x, p0, p1, p2, h0, w_in, conv_w, conv_b, w_gate, b_a, b_x, lam):
    m = x.shape[0]
    args = (x, p0, p1, p2, h0, w_in, conv_w, conv_b, w_gate, b_a, b_x, lam)
    row = lambda dt: jax.ShapeDtypeStruct((m, D_MODEL), dt)
    return pl.pallas_call(
        _front_sample_kernel,
        grid=(1,),
        in_specs=[_const_spec(a.shape) for a in args],
        out_specs=tuple(pl.BlockSpec((m, D_MODEL), lambda r: (0, 0)) for _ in range(8)),
        out_shape=(row(BF16), row(F32), row(F32), row(F32), row(F32), row(F32), row(F32), row(F32)),
        compiler_params=pltpu.CompilerParams(dimension_semantics=("arbitrary",),
                                             vmem_limit_bytes=VMEM_LIMIT),
        name="front_sample",
    )(*args)


def _sample_attn_kernel(pt_ref, ksum_ref, q_ref, kn_ref, vn_ref, ck_ref, cv_ref, o_ref,
                        kbuf, vbuf, ksem, vsem):
    b = pl.program_id(0)
    n_seq = pl.num_programs(0)
    n_blocks = ksum_ref.shape[0]
    n_gather = MOBA_TOPK * PAGES_PER_BLOCK
    par = lax.rem(b, 2)

    gate = jnp.sum(ksum_ref[...] * (1.0 / MOBA_BLOCK) * q_ref[b][None], axis=-1, keepdims=True)
    blk = lax.broadcasted_iota(jnp.int32, gate.shape, 0)

    def gather_copy(src_ref, dst_ref, sem, page, h, slot, buf):
        return pltpu.make_async_copy(src_ref.at[0, page, :, h, :], dst_ref.at[buf, h, slot], sem.at[buf])

    for t in range(MOBA_TOPK):
        mx = jnp.max(gate, axis=0, keepdims=True)
        idx = jnp.min(jnp.where(gate == mx, blk, n_blocks), axis=0, keepdims=True)
        gate = jnp.where(blk == idx, -jnp.inf, gate)
        for h in range(N_HEADS):
            sel = idx[0, h, 0]
            for pg in range(PAGES_PER_BLOCK):
                page = pt_ref[b, sel * PAGES_PER_BLOCK + pg]
                slot = t * PAGES_PER_BLOCK + pg
                gather_copy(ck_ref, kbuf, ksem, page, h, slot, par).start()
                gather_copy(cv_ref, vbuf, vsem, page, h, slot, par).start()

    def attend(seq, buf):
        for h in range(N_HEADS):
            for slot in range(n_gather):
                gather_copy(ck_ref, kbuf, ksem, 0, h, slot, buf).wait()
                gather_copy(cv_ref, vbuf, vsem, 0, h, slot, buf).wait()
        q = q_ref[seq]
        kn = kn_ref[seq]
        vn = vn_ref[seq]
        for h in range(N_HEADS):
            qh = q[h:h + 1, :]
            kh = kbuf[buf, h].reshape(n_gather * PAGE_SIZE, HEAD_DIM)
            vh = vbuf[buf, h].reshape(n_gather * PAGE_SIZE, HEAD_DIM)
            s = jnp.sum(kh * qh, axis=-1, keepdims=True) * SCALE
            s_new = jnp.sum(kn[h:h + 1, :] * qh, axis=-1, keepdims=True) * SCALE
            m = jnp.maximum(jnp.max(s, axis=0, keepdims=True), s_new)
            p = jnp.exp(s - m)
            p_new = jnp.exp(s_new - m)
            l = jnp.sum(p, axis=0, keepdims=True) + p_new
            o = jnp.sum(p * vh, axis=0, keepdims=True) + p_new * vn[h:h + 1, :]
            o_ref[seq, h:h + 1, :] = o / l

    @pl.when(b > 0)
    def _():
        attend(b - 1, 1 - par)

    @pl.when(b == n_seq - 1)
    def _():
        attend(b, par)


def _sample_attn(page_table, ksum, q, k_new, v_new, cache_k, cache_v):
    n_seq, n_pages = page_table.shape
    assert (n_pages * PAGE_SIZE) % MOBA_BLOCK == 0, "own-block cached prefix is not supported"
    n_blocks = n_pages // PAGES_PER_BLOCK
    assert n_blocks >= MOBA_TOPK and ksum.shape[0] == n_seq * n_blocks
    n_gather = MOBA_TOPK * PAGES_PER_BLOCK
    whole = pl.BlockSpec((n_seq, N_HEADS, HEAD_DIM), lambda s, pt: (0, 0, 0))
    anyspec = pl.BlockSpec(memory_space=pl.ANY)
    gathered = pltpu.VMEM((2, N_HEADS, n_gather, PAGE_SIZE, HEAD_DIM), F32)
    grid_spec = pltpu.PrefetchScalarGridSpec(
        num_scalar_prefetch=1,
        grid=(n_seq,),
        in_specs=[pl.BlockSpec((n_blocks, N_HEADS, HEAD_DIM), lambda s, pt: (s, 0, 0)),
                  whole, whole, whole, anyspec, anyspec],
        out_specs=whole,
        scratch_shapes=[gathered, gathered,
                        pltpu.SemaphoreType.DMA((2,)),
                        pltpu.SemaphoreType.DMA((2,))])
    return pl.pallas_call(
        _sample_attn_kernel,
        grid_spec=grid_spec,
        out_shape=jax.ShapeDtypeStruct((n_seq, N_HEADS, HEAD_DIM), F32),
        compiler_params=pltpu.CompilerParams(dimension_semantics=("arbitrary",),
                                             vmem_limit_bytes=VMEM_LIMIT),
        name="attn_sample",
    )(page_table, ksum, q, k_new, v_new, cache_k, cache_v)


def kernel(x_prompt, x_sample, cache_k, cache_v, state_h, state_conv, page_table, w_in, conv_w, conv_b,
           w_rg_a, b_rg_a, w_rg_x, b_rg_x, lru_lambda, w_br_rnn, w_br_attn, w_out, ln1_g, ln1_b,
           w_up, b_up, w_down, b_down, ln2_g, ln2_b):
    assert w_in.shape[0] == 1, "single-layer trunk"
    b, t, _ = x_prompt.shape
    db = x_sample.shape[0]
    assert x_sample.shape[1] == 1 and t % MOBA_BLOCK == 0

    w_in_b = w_in[0].astype(BF16)
    w_gate = jnp.concatenate([w_rg_a[0], w_rg_x[0]], axis=-1).astype(BF16)
    cw, cb = conv_w[0], conv_b
    post_w = (w_br_rnn[0].astype(BF16), w_br_attn[0].astype(BF16), w_out[0].astype(BF16),
              w_up[0].astype(BF16), w_down[0].astype(BF16), ln1_g, ln1_b, b_up, b_down, ln2_g, ln2_b)

    (rnn_p, qt_p, k_p, v_p, kb_p, vt_p, sga_p, sgb_p, selt_p, h_p, cs_p, ksum_a) = _front_prompt(
        x_prompt, w_in_b, cw, cb, w_gate, b_rg_a, b_rg_x, lru_lambda, page_table, cache_k)
    o_p = _attn_prompt(qt_p, kb_p, vt_p, selt_p)
    flat = lambda a: a.reshape(b * t, D_MODEL)
    y_p, ksum_b = _post(flat(rnn_p), o_p, flat(sga_p), flat(sgb_p), flat(x_prompt), *post_w, tm=256,
                        stream=(page_table, cache_k, ksum_a.shape[0] * PAGES_PER_BLOCK))
    ksum = jnp.concatenate([ksum_a, ksum_b], axis=0)

    xs = x_sample.reshape(db, D_MODEL)
    sc = state_conv[0]
    (rnn_s, q_s, k_s, v_s, sga_s, sgb_s, h_s, u_s) = _front_sample(
        xs, sc[:, 0], sc[:, 1], sc[:, 2], state_h[0], w_in_b, cw, cb, w_gate, b_rg_a, b_rg_x, lru_lambda)
    heads = lambda a: a.reshape(db, N_HEADS, HEAD_DIM)
    o_s = _sample_attn(page_table, ksum, heads(q_s), heads(k_s), heads(v_s), cache_k, cache_v)
    o_s_hm = jnp.transpose(o_s, (1, 0, 2)).astype(BF16)[None]
    y_s = _post(rnn_s, o_s_hm, sga_s, sgb_s, xs, *post_w, tm=db)
    cs_s = jnp.concatenate([sc[:, 1:], u_s[:, None, :]], axis=1)

    kv_p = lambda a: a.reshape(1, b, t, N_HEADS, HEAD_DIM)
    kv_s = lambda a: a.reshape(1, db, 1, N_HEADS, HEAD_DIM)
    return (y_p.reshape(b, t, D_MODEL), y_s.reshape(db, 1, D_MODEL), kv_p(k_p), kv_p(v_p),
            h_p.reshape(1, b, D_MODEL), cs_p[None],
            kv_s(k_s), kv_s(v_s), h_s[None], cs_s[None])
```

```python
import functools
import math

import jax
import jax.numpy as jnp
from jax import lax
from jax.experimental import pallas as pl
from jax.experimental.pallas import tpu as pltpu

F32 = jnp.float32
BF16 = jnp.bfloat16

D_MODEL = 1024
N_HEADS = 8
HEAD_DIM = 128
RNN_BLOCKS = 8
RNN_BW = 128
CONV_W = 4
LRU_C = 8.0
MOBA_BLOCK = 256
MOBA_TOPK = 3
PAGE_SIZE = 128
PAGES_PER_BLOCK = MOBA_BLOCK // PAGE_SIZE
D_FF = 4096
ALPHA = 2.0 ** 0.25
LN_EPS = 1e-5
NEG = -1e30
SCALE = HEAD_DIM ** -0.5
LOG2E = math.log2(math.e)

SUBLANES = 8
VMEM_LIMIT = 56 * 1024 * 1024
HEADS_PER_ITER = 4


def _sigmoid(x):
    return 0.5 * jnp.tanh(0.5 * x) + 0.5


def _softplus(x):
    return jnp.maximum(x, 0.0) + jnp.log1p(jnp.exp(-jnp.abs(x)))


def _gelu_tanh(x):
    c = math.sqrt(2.0 / math.pi)
    return x * (0.5 * (1.0 + jnp.tanh(c * (x + 0.044715 * (x * x * x)))))


def _layer_norm(x, g, b):
    mu = jnp.mean(x, axis=-1, keepdims=True)
    xc = x - mu
    var = jnp.mean(xc * xc, axis=-1, keepdims=True)
    return xc * lax.rsqrt(var + LN_EPS) * g + b


def _split_bf16(x):
    hi = x.astype(BF16)
    lo = (x - hi.astype(F32)).astype(BF16)
    return hi, lo


def _rglru_gate_dot(ucb_blk, wg):
    return jnp.dot(ucb_blk, wg, preferred_element_type=F32)


def _rglru_block(uc_blk, ucb_blk, wg, ba, bx, neg_c_sp):
    return _rglru_elem(_rglru_gate_dot(ucb_blk, wg), uc_blk, ba, bx, neg_c_sp)


def _rglru_elem(gz, uc_blk, ba, bx, neg_c_sp):
    r = _sigmoid(gz[:, :RNN_BW] + ba)
    ig = _sigmoid(gz[:, RNN_BW:] + bx)
    log_a = (LRU_C * r) * neg_c_sp
    a = jnp.exp(log_a)
    t = jnp.tanh(log_a)
    mult = jnp.sqrt(-2.0 * t / (1.0 - t))
    return a, mult * (ig * uc_blk)


def _page_sum(page):
    rows = page.shape[0]
    parts = jnp.sum(page.reshape(SUBLANES, rows // SUBLANES, N_HEADS, HEAD_DIM), axis=1)
    return jnp.sum(parts, axis=0)


class _PageStream:
    def __init__(self, pt_ref, ck_ref, ring, sems, ksum_ref, step, n_steps, page_base, n_chunks):
        self.pt_ref, self.ck_ref, self.ring, self.sems, self.ksum_ref = pt_ref, ck_ref, ring, sems, ksum_ref
        self.step, self.n_steps, self.page_base, self.n_chunks = step, n_steps, page_base, n_chunks
        self.n_ring = ring.shape[0]
        self.half = self.n_ring // 2
        self.per_step = ksum_ref.shape[0] * PAGES_PER_BLOCK
        assert self.per_step == n_chunks * self.half and n_chunks % 2 == 0
        assert self.half % PAGES_PER_BLOCK == 0
        self.span = self.per_step * n_steps

    def _copy(self, rel, slot):
        n_pages = self.pt_ref.shape[1]
        g = self.page_base + lax.rem(jnp.asarray(rel, jnp.int32), jnp.asarray(self.span, jnp.int32))
        page = self.pt_ref[g // n_pages, lax.rem(g, n_pages)]
        return pltpu.make_async_copy(self.ck_ref.at[0, page], self.ring.at[slot],
                                     self.sems.at[slot // self.half])

    def _slot(self, c, s):
        return (c % 2) * self.half + s

    def prime(self):
        @pl.when(self.step == 0)
        def _():
            for s in range(self.n_ring):
                self._copy(s, s).start()

    def begin(self, c):
        for s in range(self.half):
            self._copy(0, self._slot(c, s)).wait()

    def sums(self, c, part, n_parts):
        blocks = self.half // PAGES_PER_BLOCK
        assert blocks % n_parts == 0
        for blk in range(part * blocks // n_parts, (part + 1) * blocks // n_parts):
            acc = _page_sum(self.ring[self._slot(c, blk * PAGES_PER_BLOCK)])
            for t in range(1, PAGES_PER_BLOCK):
                acc = acc + _page_sum(self.ring[self._slot(c, blk * PAGES_PER_BLOCK + t)])
            self.ksum_ref[c * blocks + blk] = acc

    def end(self, c):
        for s in range(self.half):
            rel = self.step * self.per_step + c * self.half + s + self.n_ring
            self._copy(rel, self._slot(c, s)).start()

    def drain(self):
        @pl.when(self.step == self.n_steps - 1)
        def _():
            for s in range(self.n_ring):
                self._copy(0, s).wait()


class _NoStream:
    def prime(self): pass
    def begin(self, c): pass
    def sums(self, c, part, n_parts): pass
    def end(self, c): pass
    def drain(self): pass


STREAM_CHUNKS = 4
FRONT_RING = 16
POST_RING = 16


def _front_kernel(pt_ref, x_ref, ck_ref, w_ref, cw_ref, cb_ref, wg_ref, ba_ref, bx_ref, lam_ref,
                  rnn_ref, qt_ref, k_ref, v_ref, kb_ref, vt_ref, sga_ref, sgb_ref, selt_ref,
                  h_ref, cs_ref, ksum_ref,
                  ubuf, abuf, sbuf, hbuf, hcar, kmt, ring, sems):
    i = pl.program_id(1)
    tt = x_ref.shape[0]
    n_blk = kmt.shape[0] // N_HEADS
    step = pl.program_id(0) * pl.num_programs(1) + i
    stream = _PageStream(pt_ref, ck_ref, ring, sems, ksum_ref, step,
                         pl.num_programs(0) * pl.num_programs(1), 0, STREAM_CHUNKS)
    stream.prime()
    stream.begin(0)

    @pl.when(i == 0)
    def _():
        ubuf[0:SUBLANES, :] = jnp.zeros((SUBLANES, D_MODEL), F32)
        hcar[...] = jnp.zeros(hcar.shape, F32)
        kmt[...] = jnp.zeros(kmt.shape, F32)

    xb = x_ref[...].astype(BF16)

    def proj(j):
        return jnp.dot(xb, w_ref[:, j * D_MODEL:(j + 1) * D_MODEL], preferred_element_type=F32)


    def gate_dots(ucb, lo, hi):
        return [_rglru_gate_dot(ucb[:, n * RNN_BW:(n + 1) * RNN_BW], wg_ref[n]) for n in range(lo, hi)]

    def rglru_elem(gzs, uc, neg_c_sp, lo):
        for k, gz in enumerate(gzs):
            sl = slice((lo + k) * RNN_BW, (lo + k + 1) * RNN_BW)
            a, uin = _rglru_elem(gz, uc[:, sl], ba_ref[:, sl], bx_ref[:, sl], neg_c_sp[:, sl])
            abuf[:, sl] = a
            sbuf[:, sl] = uin

    def scan_groups(h, lo, hi):
        sub = lax.broadcasted_iota(jnp.int32, (SUBLANES, D_MODEL), 0)
        for r in range(lo, hi):
            rows = slice(r * SUBLANES, (r + 1) * SUBLANES)
            a = abuf[rows, :]
            s = sbuf[rows, :]
            for sh in (1, 2, 4):
                a_prev = jnp.where(sub >= sh, pltpu.roll(a, sh, 0), 1.0)
                s_prev = jnp.where(sub >= sh, pltpu.roll(s, sh, 0), 0.0)
                s = s + a * s_prev
                a = a * a_prev
            hs_r = s + a * h
            hbuf[rows, :] = hs_r
            h = hs_r[SUBLANES - 1:SUBLANES, :]
        return h

    def query_side(qf):
        qt = qf.T
        qt_ref[...] = (qt * (SCALE * LOG2E)).astype(BF16)
        q_hi, q_lo = _split_bf16(qt)
        km_hi, km_lo = _split_bf16(kmt[...])
        nk = kmt.shape[0]
        g2 = jnp.dot(jnp.concatenate([km_hi, km_lo], axis=0), q_hi, preferred_element_type=F32)
        gate = g2[:nk] + g2[nk:] + jnp.dot(km_hi, q_lo, preferred_element_type=F32)
        gn = [gate[n * N_HEADS:(n + 1) * N_HEADS, :] for n in range(n_blk)]
        for n in range(n_blk):
            cnt = jnp.zeros(gn[n].shape, jnp.int32)
            for m in range(n_blk):
                if m == n:
                    continue
                beats = (gn[m] >= gn[n]) if m < n else (gn[m] > gn[n])
                cnt = cnt + jnp.where(beats, 1, 0) * jnp.where(m < i, 1, 0)
            keep = jnp.where(cnt < MOBA_TOPK, 1.0, 0.0) * jnp.where(n < i, 1.0, 0.0)
            selt_ref[n * N_HEADS:(n + 1) * N_HEADS, :] = keep.astype(F32)

    def key_side(kf):
        k_ref[...] = kf
        for hd in range(N_HEADS):
            kb_ref[hd] = kf[:, hd * HEAD_DIM:(hd + 1) * HEAD_DIM].astype(BF16)
        kmean = jnp.sum(kf, axis=0, keepdims=True) * (1.0 / MOBA_BLOCK)
        head_of_lane = lax.broadcasted_iota(jnp.int32, (N_HEADS, D_MODEL), 1) // HEAD_DIM
        row = lax.broadcasted_iota(jnp.int32, (N_HEADS, D_MODEL), 0)
        kmt[pl.ds(pl.multiple_of(i * N_HEADS, N_HEADS), N_HEADS), :] = jnp.where(
            head_of_lane == row, jnp.broadcast_to(kmean, (N_HEADS, D_MODEL)), 0.0)

    u = proj(0)
    qf = proj(2)
    stream.sums(0, 0, 2)
    ubuf[SUBLANES:SUBLANES + tt, :] = u
    uc = cb_ref[...] + cw_ref[CONV_W - 1:CONV_W, :] * u
    for j in range(CONV_W - 1):
        s = SUBLANES - (CONV_W - 1) + j
        uc = uc + cw_ref[j:j + 1, :] * ubuf[s:s + tt, :]
    cs_ref[...] = ubuf[SUBLANES + tt - (CONV_W - 1):SUBLANES + tt, :]
    ubuf[0:SUBLANES, :] = ubuf[tt:tt + SUBLANES, :]
    neg_c_sp = -_softplus(-lam_ref[...])
    ucb = uc.astype(BF16)
    kf = proj(3)
    stream.sums(0, 1, 2)
    stream.end(0)

    stream.begin(1)
    query_side(qf)
    gz = gate_dots(ucb, 0, 2)
    vf = proj(4)
    stream.sums(1, 0, 2)
    key_side(kf)
    rglru_elem(gz, uc, neg_c_sp, 0)
    stream.sums(1, 1, 2)
    stream.end(1)

    stream.begin(2)
    gz = gate_dots(ucb, 2, 4)
    pre_a = proj(5)
    stream.sums(2, 0, 2)
    v_ref[...] = vf
    vt_ref[...] = vf.T.astype(BF16)
    rglru_elem(gz, uc, neg_c_sp, 2)
    gz = gate_dots(ucb, 4, 6)
    pre_b = proj(6)
    stream.sums(2, 1, 2)
    sga_ref[...] = _sigmoid(pre_a)
    rglru_elem(gz, uc, neg_c_sp, 4)
    stream.end(2)

    stream.begin(3)
    gz = gate_dots(ucb, 6, RNN_BLOCKS)
    g = proj(1)
    stream.sums(3, 0, 2)
    sgb_ref[...] = _sigmoid(pre_b)
    rglru_elem(gz, uc, neg_c_sp, 6)
    stream.sums(3, 1, 2)
    h = scan_groups(hcar[0:1, :], 0, tt // SUBLANES)
    hcar[0:1, :] = h
    h_ref[...] = h
    rnn_ref[...] = (hbuf[...] * _gelu_tanh(g)).astype(BF16)
    stream.end(3)
    stream.drain()


def _const_spec(shape):
    nd = len(shape)
    return pl.BlockSpec(shape, lambda *_: (0,) * nd, pipeline_mode=pl.Buffered(1))


def _front_prompt(x, w_in, conv_w, conv_b, w_gate, b_a, b_x, lam, page_table, cache_k):
    b, t, _ = x.shape
    tt = MOBA_BLOCK
    nt = t // tt
    blocks_per_step = STREAM_CHUNKS * (FRONT_RING // 2) // PAGES_PER_BLOCK
    tile = pl.BlockSpec((None, tt, D_MODEL), lambda bi, ti, *_: (bi, ti, 0))
    ttile = pl.BlockSpec((None, D_MODEL, tt), lambda bi, ti, *_: (bi, 0, ti))
    big = lambda dt: jax.ShapeDtypeStruct((b, t, D_MODEL), dt)
    tbig = jax.ShapeDtypeStruct((b, D_MODEL, t), BF16)
    hm_tile = pl.BlockSpec((None, N_HEADS, tt, HEAD_DIM), lambda bi, ti, *_: (bi, 0, ti, 0))
    hm_big = jax.ShapeDtypeStruct((b, N_HEADS, t, HEAD_DIM), BF16)
    out_shape = (big(BF16), tbig, big(F32), big(F32), hm_big, tbig, big(F32), big(F32),
                 jax.ShapeDtypeStruct((b, nt, nt * N_HEADS, tt), F32),
                 jax.ShapeDtypeStruct((b, 1, D_MODEL), F32),
                 jax.ShapeDtypeStruct((b, CONV_W - 1, D_MODEL), F32),
                 jax.ShapeDtypeStruct((b * nt * blocks_per_step, N_HEADS, HEAD_DIM), F32))
    out_specs = (tile, ttile, tile, tile, hm_tile, ttile, tile, tile,
                 pl.BlockSpec((None, None, nt * N_HEADS, tt), lambda bi, ti, *_: (bi, ti, 0, 0)),
                 pl.BlockSpec((None, 1, D_MODEL), lambda bi, ti, *_: (bi, 0, 0)),
                 pl.BlockSpec((None, CONV_W - 1, D_MODEL), lambda bi, ti, *_: (bi, 0, 0)),
                 pl.BlockSpec((blocks_per_step, N_HEADS, HEAD_DIM), lambda bi, ti, *_: (bi * nt + ti, 0, 0)))
    in_specs = [tile, pl.BlockSpec(memory_space=pl.ANY),
                _const_spec(w_in.shape), _const_spec(conv_w.shape), _const_spec(conv_b.shape),
                _const_spec(w_gate.shape), _const_spec(b_a.shape), _const_spec(b_x.shape), _const_spec(lam.shape)]
    grid_spec = pltpu.PrefetchScalarGridSpec(
        num_scalar_prefetch=1,
        grid=(b, nt),
        in_specs=in_specs,
        out_specs=out_specs,
        scratch_shapes=[pltpu.VMEM((SUBLANES + tt, D_MODEL), F32),
                        pltpu.VMEM((tt, D_MODEL), F32),
                        pltpu.VMEM((tt, D_MODEL), F32),
                        pltpu.VMEM((tt, D_MODEL), F32),
                        pltpu.VMEM((SUBLANES, D_MODEL), F32),
                        pltpu.VMEM((nt * N_HEADS, D_MODEL), F32),
                        pltpu.VMEM((FRONT_RING, PAGE_SIZE, N_HEADS, HEAD_DIM), F32),
                        pltpu.SemaphoreType.DMA((2,))])
    return pl.pallas_call(
        _front_kernel,
        grid_spec=grid_spec,
        out_shape=out_shape,
        compiler_params=pltpu.CompilerParams(dimension_semantics=("arbitrary", "arbitrary"),
                                             vmem_limit_bytes=VMEM_LIMIT),
        name="front_prompt",
    )(page_table, x, cache_k, w_in, conv_w, conv_b, w_gate, b_a, b_x, lam)


def _attn_kernel(qt_ref, kb_ref, vt_ref, selt_ref, o_ref):
    i = pl.program_id(1)
    tq = qt_ref.shape[1]
    n_q = kb_ref.shape[1] // MOBA_BLOCK
    key_idx = lax.broadcasted_iota(jnp.int32, (MOBA_BLOCK, tq), 0)
    qry_idx = lax.broadcasted_iota(jnp.int32, (MOBA_BLOCK, tq), 1)
    causal = key_idx <= qry_idx

    def head_rows(h):
        return pl.ds(pl.multiple_of(h * HEAD_DIM, HEAD_DIM), HEAD_DIM)

    def masked_scores(h, n_past):
        nk = (n_past + 1) * MOBA_BLOCK
        s = jnp.dot(kb_ref[h, 0:nk, :], qt_ref[head_rows(h), :], preferred_element_type=F32)
        parts = []
        for j in range(n_past):
            keep = selt_ref[pl.ds(j * N_HEADS + h, 1), :]
            parts.append(jnp.where(keep > 0.5, s[j * MOBA_BLOCK:(j + 1) * MOBA_BLOCK, :], NEG))
        parts.append(jnp.where(causal, s[n_past * MOBA_BLOCK:, :], NEG))
        return jnp.concatenate(parts, axis=0) if n_past else parts[0]

    def softmax_pv(h, s):
        nk = s.shape[0]
        m = jnp.max(s, axis=0, keepdims=True)
        p = jnp.exp2(s - m)
        l = jnp.sum(p, axis=0, keepdims=True)
        ot = jnp.dot(vt_ref[head_rows(h), 0:nk], p.astype(BF16), preferred_element_type=F32) * (1.0 / l)
        o_ref[h] = ot.T.astype(BF16)

    def attend(n_past):
        def head_group(g, carry):
            heads = [g * HEADS_PER_ITER + u for u in range(HEADS_PER_ITER)]
            scores = [masked_scores(h, n_past) for h in heads]
            for h, s in zip(heads, scores):
                softmax_pv(h, s)
            return carry
        lax.fori_loop(0, N_HEADS // HEADS_PER_ITER, head_group, 0)

    for v in range(n_q):
        @pl.when(i == v)
        def _(v=v):
            attend(v)


def _attn_prompt(qt, kb, vt, selt):
    b, _, t = qt.shape
    tq = MOBA_BLOCK
    nq = t // tq
    return pl.pallas_call(
        _attn_kernel,
        grid=(b, nq),
        in_specs=[pl.BlockSpec((None, D_MODEL, tq), lambda bi, qi: (bi, 0, qi)),
                  pl.BlockSpec((None, N_HEADS, t, HEAD_DIM), lambda bi, qi: (bi, 0, 0, 0)),
                  pl.BlockSpec((None, D_MODEL, t), lambda bi, qi: (bi, 0, 0)),
                  pl.BlockSpec((None, None, nq * N_HEADS, tq), lambda bi, qi: (bi, qi, 0, 0))],
        out_specs=pl.BlockSpec((None, N_HEADS, tq, HEAD_DIM), lambda bi, qi: (bi, 0, qi, 0)),
        out_shape=jax.ShapeDtypeStruct((b, N_HEADS, t, HEAD_DIM), BF16),
        compiler_params=pltpu.CompilerParams(dimension_semantics=("arbitrary", "arbitrary"),
                                             vmem_limit_bytes=VMEM_LIMIT),
        name="attn_prompt",
    )(qt, kb, vt, selt)


def _post_body(rnn_ref, o_ref, sga_ref, sgb_ref, x_ref, wbr_ref, wba_ref, wout_ref, wup_ref, wdn_ref,
               g1_ref, b1_ref, bup_ref, bdn_ref, g2_ref, b2_ref, y_ref, stream):
    stream.begin(0)
    y_rnn = jnp.dot(rnn_ref[...], wbr_ref[...], preferred_element_type=F32)
    stream.sums(0, 0, 2)
    o = jnp.concatenate([o_ref[h] for h in range(N_HEADS)], axis=-1)
    y_att = jnp.dot(o, wba_ref[...], preferred_element_type=F32)
    stream.sums(0, 1, 2)
    mix = sga_ref[...] * y_rnn + sgb_ref[...] * y_att
    stream.end(0)
    stream.begin(1)
    t1 = ALPHA * x_ref[...] + jnp.dot(mix.astype(BF16), wout_ref[...], preferred_element_type=F32)
    stream.sums(1, 0, 2)
    x1 = _layer_norm(t1, g1_ref[...], b1_ref[...])
    x1b = x1.astype(BF16)
    acc = ALPHA * x1 + bdn_ref[...]
    stream.sums(1, 1, 2)
    stream.end(1)
    n_chunks = D_FF // D_MODEL
    for c in range(n_chunks):
        stage, part = 2 + c // 2, c % 2
        if part == 0:
            stream.begin(stage)
        cs = slice(c * D_MODEL, (c + 1) * D_MODEL)
        hid = jnp.maximum(jnp.dot(x1b, wup_ref[:, cs], preferred_element_type=F32) + bup_ref[:, cs], 0.0)
        stream.sums(stage, 2 * part, 4)
        acc = acc + jnp.dot((hid * hid).astype(BF16), wdn_ref[cs, :], preferred_element_type=F32)
        stream.sums(stage, 2 * part + 1, 4)
        if part == 1 and stage == 2:
            stream.end(stage)
    y_ref[...] = _layer_norm(acc, g2_ref[...], b2_ref[...])
    stream.end(3)


def _post_kernel(*refs):
    _post_body(*refs, stream=_NoStream())


def _post_stream_kernel(page_base, pt_ref, rnn_ref, o_ref, sga_ref, sgb_ref, x_ref, ck_ref, *rest):
    consts, (y_ref, ksum_ref, ring, sems) = rest[:11], rest[11:]
    stream = _PageStream(pt_ref, ck_ref, ring, sems, ksum_ref, pl.program_id(0), pl.num_programs(0),
                         page_base, STREAM_CHUNKS)
    stream.prime()
    _post_body(rnn_ref, o_ref, sga_ref, sgb_ref, x_ref, *consts, y_ref, stream=stream)
    stream.drain()


def _post(rnn, o, sga, sgb, x, wbr, wba, wout, wup, wdn, g1, b1, bup, bdn, g2, b2, tm, stream=None):
    m = x.shape[0]
    n_steps = m // tm
    tiles_per_seq = o.shape[2] // tm
    assert o.shape[0] * o.shape[2] == m
    tile = pl.BlockSpec((tm, D_MODEL), lambda r, *_: (r, 0))
    o_tile = pl.BlockSpec((None, N_HEADS, tm, HEAD_DIM),
                          lambda r, *_: (r // tiles_per_seq, 0, r % tiles_per_seq, 0))
    consts = (wbr, wba, wout, wup, wdn, g1, b1, bup, bdn, g2, b2)
    params = pltpu.CompilerParams(dimension_semantics=("arbitrary",), vmem_limit_bytes=VMEM_LIMIT)
    y_shape = jax.ShapeDtypeStruct((m, D_MODEL), F32)
    if stream is None:
        return pl.pallas_call(
            _post_kernel,
            grid=(n_steps,),
            in_specs=[tile, o_tile, tile, tile, tile] + [_const_spec(c.shape) for c in consts],
            out_specs=tile,
            out_shape=y_shape,
            compiler_params=params,
            name="post_m%d" % m,
        )(rnn, o, sga, sgb, x, *consts)
    page_table, cache_k, page_base = stream
    blocks_per_step = STREAM_CHUNKS * (POST_RING // 2) // PAGES_PER_BLOCK
    n_blocks = blocks_per_step * n_steps
    assert page_base + n_blocks * PAGES_PER_BLOCK == page_table.size, "front + post shares must cover all pages"
    grid_spec = pltpu.PrefetchScalarGridSpec(
        num_scalar_prefetch=1,
        grid=(n_steps,),
        in_specs=([tile, o_tile, tile, tile, tile, pl.BlockSpec(memory_space=pl.ANY)]
                  + [_const_spec(c.shape) for c in consts]),
        out_specs=(tile, pl.BlockSpec((blocks_per_step, N_HEADS, HEAD_DIM), lambda r, *_: (r, 0, 0))),
        scratch_shapes=[pltpu.VMEM((POST_RING, PAGE_SIZE, N_HEADS, HEAD_DIM), F32),
                        pltpu.SemaphoreType.DMA((2,))])
    return pl.pallas_call(
        functools.partial(_post_stream_kernel, page_base),
        grid_spec=grid_spec,
        out_shape=(y_shape, jax.ShapeDtypeStruct((n_blocks, N_HEADS, HEAD_DIM), F32)),
        compiler_params=params,
        name="post_stream_m%d" % m,
    )(page_table, rnn, o, sga, sgb, x, cache_k, *consts)


def _front_sample_kernel(x_ref, p0_ref, p1_ref, p2_ref, h0_ref, w_ref, cw_ref, cb_ref, wg_ref, ba_ref,
                         bx_ref, lam_ref,
                         rnn_ref, q_ref, k_ref, v_ref, sga_ref, sgb_ref, h_ref, u_ref):
    xb = x_ref[...].astype(BF16)

    def proj(j):
        return jnp.dot(xb, w_ref[:, j * D_MODEL:(j + 1) * D_MODEL], preferred_element_type=F32)

    u = proj(0)
    u_ref[...] = u
    uc = (cb_ref[...] + cw_ref[0:1, :] * p0_ref[...] + cw_ref[1:2, :] * p1_ref[...]
          + cw_ref[2:3, :] * p2_ref[...] + cw_ref[3:4, :] * u)
    neg_c_sp = -_softplus(-lam_ref[...])
    ucb = uc.astype(BF16)
    g = proj(1)
    for n in range(RNN_BLOCKS):
        sl = slice(n * RNN_BW, (n + 1) * RNN_BW)
        a, uin = _rglru_block(uc[:, sl], ucb[:, sl], wg_ref[n], ba_ref[:, sl], bx_ref[:, sl],
                              neg_c_sp[:, sl])
        h = a * h0_ref[:, sl] + uin
        h_ref[:, sl] = h
        rnn_ref[:, sl] = (h * _gelu_tanh(g[:, sl])).astype(BF16)
    q_ref[...] = proj(2)
    k_ref[...] = proj(3)
    v_ref[...] = proj(4)
    sga_ref[...] = _sigmoid(proj(5))
    sgb_ref[...] = _sigmoid(proj(6))


def _front_sample(x, p0, p1, p2, h0, w_in, conv_w, conv_b, w_gate, b_a, b_x, lam):
    m = x.shape[0]
    args = (x, p0, p1, p2, h0, w_in, conv_w, conv_b, w_gate, b_a, b_x, lam)
    row = lambda dt: jax.ShapeDtypeStruct((m, D_MODEL), dt)
    return pl.pallas_call(
        _front_sample_kernel,
        grid=(1,),
        in_specs=[_const_spec(a.shape) for a in args],
        out_specs=tuple(pl.BlockSpec((m, D_MODEL), lambda r: (0, 0)) for _ in range(8)),
        out_shape=(row(BF16), row(F32), row(F32), row(F32), row(F32), row(F32), row(F32), row(F32)),
        compiler_params=pltpu.CompilerParams(dimension_semantics=("arbitrary",),
                                             vmem_limit_bytes=VMEM_LIMIT),
        name="front_sample",
    )(*args)


def _sample_attn_kernel(pt_ref, ksum_ref, q_ref, kn_ref, vn_ref, ck_ref, cv_ref, o_ref,
                        kbuf, vbuf, ksem, vsem):
    b = pl.program_id(0)
    n_seq = pl.num_programs(0)
    n_blocks = ksum_ref.shape[0]
    n_gather = MOBA_TOPK * PAGES_PER_BLOCK
    par = lax.rem(b, 2)

    gate = jnp.sum(ksum_ref[...] * (1.0 / MOBA_BLOCK) * q_ref[b][None], axis=-1, keepdims=True)
    blk = lax.broadcasted_iota(jnp.int32, gate.shape, 0)

    def gather_copy(src_ref, dst_ref, sem, page, h, slot, buf):
        return pltpu.make_async_copy(src_ref.at[0, page, :, h, :], dst_ref.at[buf, h, slot], sem.at[buf])

    for t in range(MOBA_TOPK):
        mx = jnp.max(gate, axis=0, keepdims=True)
        idx = jnp.min(jnp.where(gate == mx, blk, n_blocks), axis=0, keepdims=True)
        gate = jnp.where(blk == idx, -jnp.inf, gate)
        for h in range(N_HEADS):
            sel = idx[0, h, 0]
            for pg in range(PAGES_PER_BLOCK):
                page = pt_ref[b, sel * PAGES_PER_BLOCK + pg]
                slot = t * PAGES_PER_BLOCK + pg
                gather_copy(ck_ref, kbuf, ksem, page, h, slot, par).start()
                gather_copy(cv_ref, vbuf, vsem, page, h, slot, par).start()

    def attend(seq, buf):
        for h in range(N_HEADS):
            for slot in range(n_gather):
                gather_copy(ck_ref, kbuf, ksem, 0, h, slot, buf).wait()
                gather_copy(cv_ref, vbuf, vsem, 0, h, slot, buf).wait()
        q = q_ref[seq]
        kn = kn_ref[seq]
        vn = vn_ref[seq]
        for h in range(N_HEADS):
            qh = q[h:h + 1, :]
            kh = kbuf[buf, h].reshape(n_gather * PAGE_SIZE, HEAD_DIM)
            vh = vbuf[buf, h].reshape(n_gather * PAGE_SIZE, HEAD_DIM)
            s = jnp.sum(kh * qh, axis=-1, keepdims=True) * SCALE
            s_new = jnp.sum(kn[h:h + 1, :] * qh, axis=-1, keepdims=True) * SCALE
            m = jnp.maximum(jnp.max(s, axis=0, keepdims=True), s_new)
            p = jnp.exp(s - m)
            p_new = jnp.exp(s_new - m)
            l = jnp.sum(p, axis=0, keepdims=True) + p_new
            o = jnp.sum(p * vh, axis=0, keepdims=True) + p_new * vn[h:h + 1, :]
            o_ref[seq, h:h + 1, :] = o / l

    @pl.when(b > 0)
    def _():
        attend(b - 1, 1 - par)

    @pl.when(b == n_seq - 1)
    def _():
        attend(b, par)


def _sample_attn(page_table, ksum, q, k_new, v_new, cache_k, cache_v):
    n_seq, n_pages = page_table.shape
    assert (n_pages * PAGE_SIZE) % MOBA_BLOCK == 0, "own-block cached prefix is not supported"
    n_blocks = n_pages // PAGES_PER_BLOCK
    assert n_blocks >= MOBA_TOPK and ksum.shape[0] == n_seq * n_blocks
    n_gather = MOBA_TOPK * PAGES_PER_BLOCK
    whole = pl.BlockSpec((n_seq, N_HEADS, HEAD_DIM), lambda s, pt: (0, 0, 0))
    anyspec = pl.BlockSpec(memory_space=pl.ANY)
    gathered = pltpu.VMEM((2, N_HEADS, n_gather, PAGE_SIZE, HEAD_DIM), F32)
    grid_spec = pltpu.PrefetchScalarGridSpec(
        num_scalar_prefetch=1,
        grid=(n_seq,),
        in_specs=[pl.BlockSpec((n_blocks, N_HEADS, HEAD_DIM), lambda s, pt: (s, 0, 0)),
                  whole, whole, whole, anyspec, anyspec],
        out_specs=whole,
        scratch_shapes=[gathered, gathered,
                        pltpu.SemaphoreType.DMA((2,)),
                        pltpu.SemaphoreType.DMA((2,))])
    return pl.pallas_call(
        _sample_attn_kernel,
        grid_spec=grid_spec,
        out_shape=jax.ShapeDtypeStruct((n_seq, N_HEADS, HEAD_DIM), F32),
        compiler_params=pltpu.CompilerParams(dimension_semantics=("arbitrary",),
                                             vmem_limit_bytes=VMEM_LIMIT),
        name="attn_sample",
    )(page_table, ksum, q, k_new, v_new, cache_k, cache_v)


def kernel(x_prompt, x_sample, cache_k, cache_v, state_h, state_conv, page_table, w_in, conv_w, conv_b,
           w_rg_a, b_rg_a, w_rg_x, b_rg_x, lru_lambda, w_br_rnn, w_br_attn, w_out, ln1_g, ln1_b,
           w_up, b_up, w_down, b_down, ln2_g, ln2_b):
    assert w_in.shape[0] == 1, "single-layer trunk"
    b, t, _ = x_prompt.shape
    db = x_sample.shape[0]
    assert x_sample.shape[1] == 1 and t % MOBA_BLOCK == 0

    w_in_b = w_in[0].astype(BF16)
    w_gate = jnp.concatenate([w_rg_a[0], w_rg_x[0]], axis=-1).astype(BF16)
    cw, cb = conv_w[0], conv_b
    post_w = (w_br_rnn[0].astype(BF16), w_br_attn[0].astype(BF16), w_out[0].astype(BF16),
              w_up[0].astype(BF16), w_down[0].astype(BF16), ln1_g, ln1_b, b_up, b_down, ln2_g, ln2_b)

    (rnn_p, qt_p, k_p, v_p, kb_p, vt_p, sga_p, sgb_p, selt_p, h_p, cs_p, ksum_a) = _front_prompt(
        x_prompt, w_in_b, cw, cb, w_gate, b_rg_a, b_rg_x, lru_lambda, page_table, cache_k)
    o_p = _attn_prompt(qt_p, kb_p, vt_p, selt_p)
    flat = lambda a: a.reshape(b * t, D_MODEL)
    y_p, ksum_b = _post(flat(rnn_p), o_p, flat(sga_p), flat(sgb_p), flat(x_prompt), *post_w, tm=256,
                        stream=(page_table, cache_k, ksum_a.shape[0] * PAGES_PER_BLOCK))
    ksum = jnp.concatenate([ksum_a, ksum_b], axis=0)

    xs = x_sample.reshape(db, D_MODEL)
    sc = state_conv[0]
    (rnn_s, q_s, k_s, v_s, sga_s, sgb_s, h_s, u_s) = _front_sample(
        xs, sc[:, 0], sc[:, 1], sc[:, 2], state_h[0], w_in_b, cw, cb, w_gate, b_rg_a, b_rg_x, lru_lambda)
    heads = lambda a: a.reshape(db, N_HEADS, HEAD_DIM)
    o_s = _sample_attn(page_table, ksum, heads(q_s), heads(k_s), heads(v_s), cache_k, cache_v)
    o_s_hm = jnp.transpose(o_s, (1, 0, 2)).astype(BF16)[None]
    y_s = _post(rnn_s, o_s_hm, sga_s, sgb_s, xs, *post_w, tm=db)
    cs_s = jnp.concatenate([sc[:, 1:], u_s[:, None, :]], axis=1)

    kv_p = lambda a: a.reshape(1, b, t, N_HEADS, HEAD_DIM)
    kv_s = lambda a: a.reshape(1, db, 1, N_HEADS, HEAD_DIM)
    return (y_p.reshape(b, t, D_MODEL), y_s.reshape(db, 1, D_MODEL), kv_p(k_p), kv_p(v_p),
            h_p.reshape(1, b, D_MODEL), cs_p[None],
            kv_s(k_s), kv_s(v_s), h_s[None], cs_s[None])
```

```python
import functools
import math

import jax
import jax.numpy as jnp
from jax import lax
from jax.experimental import pallas as pl
from jax.experimental.pallas import tpu as pltpu

F32 = jnp.float32
BF16 = jnp.bfloat16

D_MODEL = 1024
N_HEADS = 8
HEAD_DIM = 128
RNN_BLOCKS = 8
RNN_BW = 128
CONV_W = 4
LRU_C = 8.0
MOBA_BLOCK = 256
MOBA_TOPK = 3
PAGE_SIZE = 128
PAGES_PER_BLOCK = MOBA_BLOCK // PAGE_SIZE
D_FF = 4096
ALPHA = 2.0 ** 0.25
LN_EPS = 1e-5
NEG = -1e30
SCALE = HEAD_DIM ** -0.5
LOG2E = math.log2(math.e)

SUBLANES = 8
VMEM_LIMIT = 56 * 1024 * 1024
HEADS_PER_ITER = 4


def _sigmoid(x):
    return 0.5 * jnp.tanh(0.5 * x) + 0.5


def _softplus(x):
    return jnp.maximum(x, 0.0) + jnp.log1p(jnp.exp(-jnp.abs(x)))


def _gelu_tanh(x):
    c = math.sqrt(2.0 / math.pi)
    return x * (0.5 * (1.0 + jnp.tanh(c * (x + 0.044715 * (x * x * x)))))


def _layer_norm(x, g, b):
    mu = jnp.mean(x, axis=-1, keepdims=True)
    xc = x - mu
    var = jnp.mean(xc * xc, axis=-1, keepdims=True)
    return xc * lax.rsqrt(var + LN_EPS) * g + b


def _split_bf16(x):
    hi = x.astype(BF16)
    lo = (x - hi.astype(F32)).astype(BF16)
    return hi, lo


def _rglru_gate_dot(ucb_blk, wg):
    return jnp.dot(ucb_blk, wg, preferred_element_type=F32)


def _rglru_block(uc_blk, ucb_blk, wg, ba, bx, neg_c_sp):
    return _rglru_elem(_rglru_gate_dot(ucb_blk, wg), uc_blk, ba, bx, neg_c_sp)


def _rglru_elem(gz, uc_blk, ba, bx, neg_c_sp):
    r = _sigmoid(gz[:, :RNN_BW] + ba)
    ig = _sigmoid(gz[:, RNN_BW:] + bx)
    log_a = (LRU_C * r) * neg_c_sp
    a = jnp.exp(log_a)
    t = jnp.tanh(log_a)
    mult = jnp.sqrt(-2.0 * t / (1.0 - t))
    return a, mult * (ig * uc_blk)


def _page_sum(page):
    rows = page.shape[0]
    parts = jnp.sum(page.reshape(SUBLANES, rows // SUBLANES, N_HEADS, HEAD_DIM), axis=1)
    return jnp.sum(parts, axis=0)


class _PageStream:
    def __init__(self, pt_ref, ck_ref, ring, sems, ksum_ref, step, n_steps, page_base, n_chunks):
        self.pt_ref, self.ck_ref, self.ring, self.sems, self.ksum_ref = pt_ref, ck_ref, ring, sems, ksum_ref
        self.step, self.n_steps, self.page_base, self.n_chunks = step, n_steps, page_base, n_chunks
        self.n_ring = ring.shape[0]
        self.half = self.n_ring // 2
        self.per_step = ksum_ref.shape[0] * PAGES_PER_BLOCK
        assert self.per_step == n_chunks * self.half and n_chunks % 2 == 0
        assert self.half % PAGES_PER_BLOCK == 0
        self.span = self.per_step * n_steps

    def _copy(self, rel, slot):
        n_pages = self.pt_ref.shape[1]
        g = self.page_base + lax.rem(jnp.asarray(rel, jnp.int32), jnp.asarray(self.span, jnp.int32))
        page = self.pt_ref[g // n_pages, lax.rem(g, n_pages)]
        return pltpu.make_async_copy(self.ck_ref.at[0, page], self.ring.at[slot],
                                     self.sems.at[slot // self.half])

    def _slot(self, c, s):
        return (c % 2) * self.half + s

    def prime(self):
        @pl.when(self.step == 0)
        def _():
            for s in range(self.n_ring):
                self._copy(s, s).start(priority=STREAM_DMA_PRIORITY)

    def begin(self, c):
        for s in range(self.half):
            self._copy(0, self._slot(c, s)).wait()

    def sums(self, c, part, n_parts):
        blocks = self.half // PAGES_PER_BLOCK
        assert blocks % n_parts == 0
        for blk in range(part * blocks // n_parts, (part + 1) * blocks // n_parts):
            acc = _page_sum(self.ring[self._slot(c, blk * PAGES_PER_BLOCK)])
            for t in range(1, PAGES_PER_BLOCK):
                acc = acc + _page_sum(self.ring[self._slot(c, blk * PAGES_PER_BLOCK + t)])
            self.ksum_ref[c * blocks + blk] = acc

    def end(self, c):
        for s in range(self.half):
            rel = self.step * self.per_step + c * self.half + s + self.n_ring
            self._copy(rel, self._slot(c, s)).start(priority=STREAM_DMA_PRIORITY)

    def drain(self):
        @pl.when(self.step == self.n_steps - 1)
        def _():
            for s in range(self.n_ring):
                self._copy(0, s).wait()


class _NoStream:
    def prime(self): pass
    def begin(self, c): pass
    def sums(self, c, part, n_parts): pass
    def end(self, c): pass
    def drain(self): pass


STREAM_CHUNKS = 4
STREAM_DMA_PRIORITY = 1
FRONT_RING = 16
POST_RING = 16


def _front_kernel(pt_ref, x_ref, ck_ref, w_ref, cw_ref, cb_ref, wg_ref, ba_ref, bx_ref, lam_ref,
                  rnn_ref, qt_ref, k_ref, v_ref, kb_ref, vt_ref, sga_ref, sgb_ref, selt_ref,
                  h_ref, cs_ref, ksum_ref,
                  ubuf, abuf, sbuf, hbuf, hcar, kmt, ring, sems):
    i = pl.program_id(1)
    tt = x_ref.shape[0]
    n_blk = kmt.shape[0] // N_HEADS
    step = pl.program_id(0) * pl.num_programs(1) + i
    stream = _PageStream(pt_ref, ck_ref, ring, sems, ksum_ref, step,
                         pl.num_programs(0) * pl.num_programs(1), 0, STREAM_CHUNKS)
    stream.prime()
    stream.begin(0)

    @pl.when(i == 0)
    def _():
        ubuf[0:SUBLANES, :] = jnp.zeros((SUBLANES, D_MODEL), F32)
        hcar[...] = jnp.zeros(hcar.shape, F32)
        kmt[...] = jnp.zeros(kmt.shape, F32)

    xb = x_ref[...].astype(BF16)

    def proj(j):
        return jnp.dot(xb, w_ref[:, j * D_MODEL:(j + 1) * D_MODEL], preferred_element_type=F32)


    def gate_dots(ucb, lo, hi):
        return [_rglru_gate_dot(ucb[:, n * RNN_BW:(n + 1) * RNN_BW], wg_ref[n]) for n in range(lo, hi)]

    def rglru_elem(gzs, uc, neg_c_sp, lo):
        for k, gz in enumerate(gzs):
            sl = slice((lo + k) * RNN_BW, (lo + k + 1) * RNN_BW)
            a, uin = _rglru_elem(gz, uc[:, sl], ba_ref[:, sl], bx_ref[:, sl], neg_c_sp[:, sl])
            abuf[:, sl] = a
            sbuf[:, sl] = uin

    def scan_groups(h, lo, hi):
        sub = lax.broadcasted_iota(jnp.int32, (SUBLANES, D_MODEL), 0)
        for r in range(lo, hi):
            rows = slice(r * SUBLANES, (r + 1) * SUBLANES)
            a = abuf[rows, :]
            s = sbuf[rows, :]
            for sh in (1, 2, 4):
                a_prev = jnp.where(sub >= sh, pltpu.roll(a, sh, 0), 1.0)
                s_prev = jnp.where(sub >= sh, pltpu.roll(s, sh, 0), 0.0)
                s = s + a * s_prev
                a = a * a_prev
            hs_r = s + a * h
            hbuf[rows, :] = hs_r
            h = hs_r[SUBLANES - 1:SUBLANES, :]
        return h

    def query_side(qf):
        qt = qf.T
        qt_ref[...] = (qt * (SCALE * LOG2E)).astype(BF16)
        q_hi, q_lo = _split_bf16(qt)
        km_hi, km_lo = _split_bf16(kmt[...])
        nk = kmt.shape[0]
        g2 = jnp.dot(jnp.concatenate([km_hi, km_lo], axis=0), q_hi, preferred_element_type=F32)
        gate = g2[:nk] + g2[nk:] + jnp.dot(km_hi, q_lo, preferred_element_type=F32)
        gn = [gate[n * N_HEADS:(n + 1) * N_HEADS, :] for n in range(n_blk)]
        for n in range(n_blk):
            cnt = jnp.zeros(gn[n].shape, jnp.int32)
            for m in range(n_blk):
                if m == n:
                    continue
                beats = (gn[m] >= gn[n]) if m < n else (gn[m] > gn[n])
                cnt = cnt + jnp.where(beats, 1, 0) * jnp.where(m < i, 1, 0)
            keep = jnp.where(cnt < MOBA_TOPK, 1.0, 0.0) * jnp.where(n < i, 1.0, 0.0)
            selt_ref[n * N_HEADS:(n + 1) * N_HEADS, :] = keep.astype(F32)

    def key_side(kf):
        k_ref[...] = kf
        for hd in range(N_HEADS):
            kb_ref[hd] = kf[:, hd * HEAD_DIM:(hd + 1) * HEAD_DIM].astype(BF16)
        kmean = jnp.sum(kf, axis=0, keepdims=True) * (1.0 / MOBA_BLOCK)
        head_of_lane = lax.broadcasted_iota(jnp.int32, (N_HEADS, D_MODEL), 1) // HEAD_DIM
        row = lax.broadcasted_iota(jnp.int32, (N_HEADS, D_MODEL), 0)
        kmt[pl.ds(pl.multiple_of(i * N_HEADS, N_HEADS), N_HEADS), :] = jnp.where(
            head_of_lane == row, jnp.broadcast_to(kmean, (N_HEADS, D_MODEL)), 0.0)

    u = proj(0)
    qf = proj(2)
    stream.sums(0, 0, 2)
    ubuf[SUBLANES:SUBLANES + tt, :] = u
    uc = cb_ref[...] + cw_ref[CONV_W - 1:CONV_W, :] * u
    for j in range(CONV_W - 1):
        s = SUBLANES - (CONV_W - 1) + j
        uc = uc + cw_ref[j:j + 1, :] * ubuf[s:s + tt, :]
    cs_ref[...] = ubuf[SUBLANES + tt - (CONV_W - 1):SUBLANES + tt, :]
    ubuf[0:SUBLANES, :] = ubuf[tt:tt + SUBLANES, :]
    neg_c_sp = -_softplus(-lam_ref[...])
    ucb = uc.astype(BF16)
    kf = proj(3)
    stream.sums(0, 1, 2)
    stream.end(0)

    stream.begin(1)
    query_side(qf)
    gz = gate_dots(ucb, 0, 2)
    vf = proj(4)
    stream.sums(1, 0, 2)
    key_side(kf)
    rglru_elem(gz, uc, neg_c_sp, 0)
    stream.sums(1, 1, 2)
    stream.end(1)

    stream.begin(2)
    gz = gate_dots(ucb, 2, 4)
    pre_a = proj(5)
    stream.sums(2, 0, 2)
    v_ref[...] = vf
    vt_ref[...] = vf.T.astype(BF16)
    rglru_elem(gz, uc, neg_c_sp, 2)
    gz = gate_dots(ucb, 4, 6)
    pre_b = proj(6)
    stream.sums(2, 1, 2)
    sga_ref[...] = _sigmoid(pre_a)
    rglru_elem(gz, uc, neg_c_sp, 4)
    stream.end(2)

    stream.begin(3)
    gz = gate_dots(ucb, 6, RNN_BLOCKS)
    g = proj(1)
    stream.sums(3, 0, 2)
    sgb_ref[...] = _sigmoid(pre_b)
    rglru_elem(gz, uc, neg_c_sp, 6)
    stream.sums(3, 1, 2)
    h = scan_groups(hcar[0:1, :], 0, tt // SUBLANES)
    hcar[0:1, :] = h
    h_ref[...] = h
    rnn_ref[...] = (hbuf[...] * _gelu_tanh(g)).astype(BF16)
    stream.end(3)
    stream.drain()


def _const_spec(shape):
    nd = len(shape)
    return pl.BlockSpec(shape, lambda *_: (0,) * nd, pipeline_mode=pl.Buffered(1))


def _front_prompt(x, w_in, conv_w, conv_b, w_gate, b_a, b_x, lam, page_table, cache_k):
    b, t, _ = x.shape
    tt = MOBA_BLOCK
    nt = t // tt
    blocks_per_step = STREAM_CHUNKS * (FRONT_RING // 2) // PAGES_PER_BLOCK
    tile = pl.BlockSpec((None, tt, D_MODEL), lambda bi, ti, *_: (bi, ti, 0))
    ttile = pl.BlockSpec((None, D_MODEL, tt), lambda bi, ti, *_: (bi, 0, ti))
    big = lambda dt: jax.ShapeDtypeStruct((b, t, D_MODEL), dt)
    tbig = jax.ShapeDtypeStruct((b, D_MODEL, t), BF16)
    hm_tile = pl.BlockSpec((None, N_HEADS, tt, HEAD_DIM), lambda bi, ti, *_: (bi, 0, ti, 0))
    hm_big = jax.ShapeDtypeStruct((b, N_HEADS, t, HEAD_DIM), BF16)
    out_shape = (big(BF16), tbig, big(F32), big(F32), hm_big, tbig, big(F32), big(F32),
                 jax.ShapeDtypeStruct((b, nt, nt * N_HEADS, tt), F32),
                 jax.ShapeDtypeStruct((b, 1, D_MODEL), F32),
                 jax.ShapeDtypeStruct((b, CONV_W - 1, D_MODEL), F32),
                 jax.ShapeDtypeStruct((b * nt * blocks_per_step, N_HEADS, HEAD_DIM), F32))
    out_specs = (tile, ttile, tile, tile, hm_tile, ttile, tile, tile,
                 pl.BlockSpec((None, None, nt * N_HEADS, tt), lambda bi, ti, *_: (bi, ti, 0, 0)),
                 pl.BlockSpec((None, 1, D_MODEL), lambda bi, ti, *_: (bi, 0, 0)),
                 pl.BlockSpec((None, CONV_W - 1, D_MODEL), lambda bi, ti, *_: (bi, 0, 0)),
                 pl.BlockSpec((blocks_per_step, N_HEADS, HEAD_DIM), lambda bi, ti, *_: (bi * nt + ti, 0, 0)))
    in_specs = [tile, pl.BlockSpec(memory_space=pl.ANY),
                _const_spec(w_in.shape), _const_spec(conv_w.shape), _const_spec(conv_b.shape),
                _const_spec(w_gate.shape), _const_spec(b_a.shape), _const_spec(b_x.shape), _const_spec(lam.shape)]
    grid_spec = pltpu.PrefetchScalarGridSpec(
        num_scalar_prefetch=1,
        grid=(b, nt),
        in_specs=in_specs,
        out_specs=out_specs,
        scratch_shapes=[pltpu.VMEM((SUBLANES + tt, D_MODEL), F32),
                        pltpu.VMEM((tt, D_MODEL), F32),
                        pltpu.VMEM((tt, D_MODEL), F32),
                        pltpu.VMEM((tt, D_MODEL), F32),
                        pltpu.VMEM((SUBLANES, D_MODEL), F32),
                        pltpu.VMEM((nt * N_HEADS, D_MODEL), F32),
                        pltpu.VMEM((FRONT_RING, PAGE_SIZE, N_HEADS, HEAD_DIM), F32),
                        pltpu.SemaphoreType.DMA((2,))])
    return pl.pallas_call(
        _front_kernel,
        grid_spec=grid_spec,
        out_shape=out_shape,
        compiler_params=pltpu.CompilerParams(dimension_semantics=("arbitrary", "arbitrary"),
                                             vmem_limit_bytes=VMEM_LIMIT),
        name="front_prompt",
    )(page_table, x, cache_k, w_in, conv_w, conv_b, w_gate, b_a, b_x, lam)


def _attn_kernel(qt_ref, kb_ref, vt_ref, selt_ref, o_ref):
    i = pl.program_id(1)
    tq = qt_ref.shape[1]
    n_q = kb_ref.shape[1] // MOBA_BLOCK
    key_idx = lax.broadcasted_iota(jnp.int32, (MOBA_BLOCK, tq), 0)
    qry_idx = lax.broadcasted_iota(jnp.int32, (MOBA_BLOCK, tq), 1)
    causal = key_idx <= qry_idx

    def head_rows(h):
        return pl.ds(pl.multiple_of(h * HEAD_DIM, HEAD_DIM), HEAD_DIM)

    def masked_scores(h, n_past):
        nk = (n_past + 1) * MOBA_BLOCK
        s = jnp.dot(kb_ref[h, 0:nk, :], qt_ref[head_rows(h), :], preferred_element_type=F32)
        parts = []
        for j in range(n_past):
            keep = selt_ref[pl.ds(j * N_HEADS + h, 1), :]
            parts.append(jnp.where(keep > 0.5, s[j * MOBA_BLOCK:(j + 1) * MOBA_BLOCK, :], NEG))
        parts.append(jnp.where(causal, s[n_past * MOBA_BLOCK:, :], NEG))
        return jnp.concatenate(parts, axis=0) if n_past else parts[0]

    def softmax_pv(h, s):
        nk = s.shape[0]
        m = jnp.max(s, axis=0, keepdims=True)
        p = jnp.exp2(s - m)
        l = jnp.sum(p, axis=0, keepdims=True)
        ot = jnp.dot(vt_ref[head_rows(h), 0:nk], p.astype(BF16), preferred_element_type=F32) * (1.0 / l)
        o_ref[h] = ot.T.astype(BF16)

    def attend(n_past):
        def head_group(g, carry):
            heads = [g * HEADS_PER_ITER + u for u in range(HEADS_PER_ITER)]
            scores = [masked_scores(h, n_past) for h in heads]
            for h, s in zip(heads, scores):
                softmax_pv(h, s)
            return carry
        lax.fori_loop(0, N_HEADS // HEADS_PER_ITER, head_group, 0)

    for v in range(n_q):
        @pl.when(i == v)
        def _(v=v):
            attend(v)


def _attn_prompt(qt, kb, vt, selt):
    b, _, t = qt.shape
    tq = MOBA_BLOCK
    nq = t // tq
    return pl.pallas_call(
        _attn_kernel,
        grid=(b, nq),
        in_specs=[pl.BlockSpec((None, D_MODEL, tq), lambda bi, qi: (bi, 0, qi)),
                  pl.BlockSpec((None, N_HEADS, t, HEAD_DIM), lambda bi, qi: (bi, 0, 0, 0)),
                  pl.BlockSpec((None, D_MODEL, t), lambda bi, qi: (bi, 0, 0)),
                  pl.BlockSpec((None, None, nq * N_HEADS, tq), lambda bi, qi: (bi, qi, 0, 0))],
        out_specs=pl.BlockSpec((None, N_HEADS, tq, HEAD_DIM), lambda bi, qi: (bi, 0, qi, 0)),
        out_shape=jax.ShapeDtypeStruct((b, N_HEADS, t, HEAD_DIM), BF16),
        compiler_params=pltpu.CompilerParams(dimension_semantics=("arbitrary", "arbitrary"),
                                             vmem_limit_bytes=VMEM_LIMIT),
        name="attn_prompt",
    )(qt, kb, vt, selt)


def _post_body(rnn_ref, o_ref, sga_ref, sgb_ref, x_ref, wbr_ref, wba_ref, wout_ref, wup_ref, wdn_ref,
               g1_ref, b1_ref, bup_ref, bdn_ref, g2_ref, b2_ref, y_ref, stream):
    stream.begin(0)
    y_rnn = jnp.dot(rnn_ref[...], wbr_ref[...], preferred_element_type=F32)
    stream.sums(0, 0, 2)
    o = jnp.concatenate([o_ref[h] for h in range(N_HEADS)], axis=-1)
    y_att = jnp.dot(o, wba_ref[...], preferred_element_type=F32)
    stream.sums(0, 1, 2)
    mix = sga_ref[...] * y_rnn + sgb_ref[...] * y_att
    stream.end(0)
    stream.begin(1)
    t1 = ALPHA * x_ref[...] + jnp.dot(mix.astype(BF16), wout_ref[...], preferred_element_type=F32)
    stream.sums(1, 0, 2)
    x1 = _layer_norm(t1, g1_ref[...], b1_ref[...])
    x1b = x1.astype(BF16)
    acc = ALPHA * x1 + bdn_ref[...]
    stream.sums(1, 1, 2)
    stream.end(1)
    n_chunks = D_FF // D_MODEL
    for c in range(n_chunks):
        stage, part = 2 + c // 2, c % 2
        if part == 0:
            stream.begin(stage)
        cs = slice(c * D_MODEL, (c + 1) * D_MODEL)
        hid = jnp.maximum(jnp.dot(x1b, wup_ref[:, cs], preferred_element_type=F32) + bup_ref[:, cs], 0.0)
        stream.sums(stage, 2 * part, 4)
        acc = acc + jnp.dot((hid * hid).astype(BF16), wdn_ref[cs, :], preferred_element_type=F32)
        stream.sums(stage, 2 * part + 1, 4)
        if part == 1 and stage == 2:
            stream.end(stage)
    y_ref[...] = _layer_norm(acc, g2_ref[...], b2_ref[...])
    stream.end(3)


def _post_kernel(*refs):
    _post_body(*refs, stream=_NoStream())


def _post_stream_kernel(page_base, pt_ref, rnn_ref, o_ref, sga_ref, sgb_ref, x_ref, ck_ref, *rest):
    consts, (y_ref, ksum_ref, ring, sems) = rest[:11], rest[11:]
    stream = _PageStream(pt_ref, ck_ref, ring, sems, ksum_ref, pl.program_id(0), pl.num_programs(0),
                         page_base, STREAM_CHUNKS)
    stream.prime()
    _post_body(rnn_ref, o_ref, sga_ref, sgb_ref, x_ref, *consts, y_ref, stream=stream)
    stream.drain()


def _post(rnn, o, sga, sgb, x, wbr, wba, wout, wup, wdn, g1, b1, bup, bdn, g2, b2, tm, stream=None):
    m = x.shape[0]
    n_steps = m // tm
    tiles_per_seq = o.shape[2] // tm
    assert o.shape[0] * o.shape[2] == m
    tile = pl.BlockSpec((tm, D_MODEL), lambda r, *_: (r, 0))
    o_tile = pl.BlockSpec((None, N_HEADS, tm, HEAD_DIM),
                          lambda r, *_: (r // tiles_per_seq, 0, r % tiles_per_seq, 0))
    consts = (wbr, wba, wout, wup, wdn, g1, b1, bup, bdn, g2, b2)
    params = pltpu.CompilerParams(dimension_semantics=("arbitrary",), vmem_limit_bytes=VMEM_LIMIT)
    y_shape = jax.ShapeDtypeStruct((m, D_MODEL), F32)
    if stream is None:
        return pl.pallas_call(
            _post_kernel,
            grid=(n_steps,),
            in_specs=[tile, o_tile, tile, tile, tile] + [_const_spec(c.shape) for c in consts],
            out_specs=tile,
            out_shape=y_shape,
            compiler_params=params,
            name="post_m%d" % m,
        )(rnn, o, sga, sgb, x, *consts)
    page_table, cache_k, page_base = stream
    blocks_per_step = STREAM_CHUNKS * (POST_RING // 2) // PAGES_PER_BLOCK
    n_blocks = blocks_per_step * n_steps
    assert page_base + n_blocks * PAGES_PER_BLOCK == page_table.size, "front + post shares must cover all pages"
    grid_spec = pltpu.PrefetchScalarGridSpec(
        num_scalar_prefetch=1,
        grid=(n_steps,),
        in_specs=([tile, o_tile, tile, tile, tile, pl.BlockSpec(memory_space=pl.ANY)]
                  + [_const_spec(c.shape) for c in consts]),
        out_specs=(tile, pl.BlockSpec((blocks_per_step, N_HEADS, HEAD_DIM), lambda r, *_: (r, 0, 0))),
        scratch_shapes=[pltpu.VMEM((POST_RING, PAGE_SIZE, N_HEADS, HEAD_DIM), F32),
                        pltpu.SemaphoreType.DMA((2,))])
    return pl.pallas_call(
        functools.partial(_post_stream_kernel, page_base),
        grid_spec=grid_spec,
        out_shape=(y_shape, jax.ShapeDtypeStruct((n_blocks, N_HEADS, HEAD_DIM), F32)),
        compiler_params=params,
        name="post_stream_m%d" % m,
    )(page_table, rnn, o, sga, sgb, x, cache_k, *consts)


def _front_sample_kernel(x_ref, p0_ref, p1_ref, p2_ref, h0_ref, w_ref, cw_ref, cb_ref, wg_ref, ba_ref,
                         bx_ref, lam_ref,
                         rnn_ref, q_ref, k_ref, v_ref, sga_ref, sgb_ref, h_ref, u_ref):
    xb = x_ref[...].astype(BF16)

    def proj(j):
        return jnp.dot(xb, w_ref[:, j * D_MODEL:(j + 1) * D_MODEL], preferred_element_type=F32)

    u = proj(0)
    u_ref[...] = u
    uc = (cb_ref[...] + cw_ref[0:1, :] * p0_ref[...] + cw_ref[1:2, :] * p1_ref[...]
          + cw_ref[2:3, :] * p2_ref[...] + cw_ref[3:4, :] * u)
    neg_c_sp = -_softplus(-lam_ref[...])
    ucb = uc.astype(BF16)
    g = proj(1)
    for n in range(RNN_BLOCKS):
        sl = slice(n * RNN_BW, (n + 1) * RNN_BW)
        a, uin = _rglru_block(uc[:, sl], ucb[:, sl], wg_ref[n], ba_ref[:, sl], bx_ref[:, sl],
                              neg_c_sp[:, sl])
        h = a * h0_ref[:, sl] + uin
        h_ref[:, sl] = h
        rnn_ref[:, sl] = (h * _gelu_tanh(g[:, sl])).astype(BF16)
    q_ref[...] = proj(2)
    k_ref[...] = proj(3)
    v_ref[...] = proj(4)
    sga_ref[...] = _sigmoid(proj(5))
    sgb_ref[...] = _sigmoid(proj(6))


def _front_sample(x, p0, p1, p2, h0, w_in, conv_w, conv_b, w_gate, b_a, b_x, lam):
    m = x.shape[0]
    args = (x, p0, p1, p2, h0, w_in, conv_w, conv_b, w_gate, b_a, b_x, lam)
    row = lambda dt: jax.ShapeDtypeStruct((m, D_MODEL), dt)
    return pl.pallas_call(
        _front_sample_kernel,
        grid=(1,),
        in_specs=[_const_spec(a.shape) for a in args],
        out_specs=tuple(pl.BlockSpec((m, D_MODEL), lambda r: (0, 0)) for _ in range(8)),
        out_shape=(row(BF16), row(F32), row(F32), row(F32), row(F32), row(F32), row(F32), row(F32)),
        compiler_params=pltpu.CompilerParams(dimension_semantics=("arbitrary",),
                                             vmem_limit_bytes=VMEM_LIMIT),
        name="front_sample",
    )(*args)


def _sample_attn_kernel(pt_ref, ksum_ref, q_ref, kn_ref, vn_ref, ck_ref, cv_ref, o_ref,
                        kbuf, vbuf, ksem, vsem):
    b = pl.program_id(0)
    n_seq = pl.num_programs(0)
    n_blocks = ksum_ref.shape[0]
    n_gather = MOBA_TOPK * PAGES_PER_BLOCK
    par = lax.rem(b, 2)

    gate = jnp.sum(ksum_ref[...] * (1.0 / MOBA_BLOCK) * q_ref[b][None], axis=-1, keepdims=True)
    blk = lax.broadcasted_iota(jnp.int32, gate.shape, 0)

    def gather_copy(src_ref, dst_ref, sem, page, h, slot, buf):
        return pltpu.make_async_copy(src_ref.at[0, page, :, h, :], dst_ref.at[buf, h, slot], sem.at[buf])

    for t in range(MOBA_TOPK):
        mx = jnp.max(gate, axis=0, keepdims=True)
        idx = jnp.min(jnp.where(gate == mx, blk, n_blocks), axis=0, keepdims=True)
        gate = jnp.where(blk == idx, -jnp.inf, gate)
        for h in range(N_HEADS):
            sel = idx[0, h, 0]
            for pg in range(PAGES_PER_BLOCK):
                page = pt_ref[b, sel * PAGES_PER_BLOCK + pg]
                slot = t * PAGES_PER_BLOCK + pg
                gather_copy(ck_ref, kbuf, ksem, page, h, slot, par).start()
                gather_copy(cv_ref, vbuf, vsem, page, h, slot, par).start()

    def attend(seq, buf):
        for h in range(N_HEADS):
            for slot in range(n_gather):
                gather_copy(ck_ref, kbuf, ksem, 0, h, slot, buf).wait()
                gather_copy(cv_ref, vbuf, vsem, 0, h, slot, buf).wait()
        q = q_ref[seq]
        kn = kn_ref[seq]
        vn = vn_ref[seq]
        for h in range(N_HEADS):
            qh = q[h:h + 1, :]
            kh = kbuf[buf, h].reshape(n_gather * PAGE_SIZE, HEAD_DIM)
            vh = vbuf[buf, h].reshape(n_gather * PAGE_SIZE, HEAD_DIM)
            s = jnp.sum(kh * qh, axis=-1, keepdims=True) * SCALE
            s_new = jnp.sum(kn[h:h + 1, :] * qh, axis=-1, keepdims=True) * SCALE
            m = jnp.maximum(jnp.max(s, axis=0, keepdims=True), s_new)
            p = jnp.exp(s - m)
            p_new = jnp.exp(s_new - m)
            l = jnp.sum(p, axis=0, keepdims=True) + p_new
            o = jnp.sum(p * vh, axis=0, keepdims=True) + p_new * vn[h:h + 1, :]
            o_ref[seq, h:h + 1, :] = o / l

    @pl.when(b > 0)
    def _():
        attend(b - 1, 1 - par)

    @pl.when(b == n_seq - 1)
    def _():
        attend(b, par)


def _sample_attn(page_table, ksum, q, k_new, v_new, cache_k, cache_v):
    n_seq, n_pages = page_table.shape
    assert (n_pages * PAGE_SIZE) % MOBA_BLOCK == 0, "own-block cached prefix is not supported"
    n_blocks = n_pages // PAGES_PER_BLOCK
    assert n_blocks >= MOBA_TOPK and ksum.shape[0] == n_seq * n_blocks
    n_gather = MOBA_TOPK * PAGES_PER_BLOCK
    whole = pl.BlockSpec((n_seq, N_HEADS, HEAD_DIM), lambda s, pt: (0, 0, 0))
    anyspec = pl.BlockSpec(memory_space=pl.ANY)
    gathered = pltpu.VMEM((2, N_HEADS, n_gather, PAGE_SIZE, HEAD_DIM), F32)
    grid_spec = pltpu.PrefetchScalarGridSpec(
        num_scalar_prefetch=1,
        grid=(n_seq,),
        in_specs=[pl.BlockSpec((n_blocks, N_HEADS, HEAD_DIM), lambda s, pt: (s, 0, 0)),
                  whole, whole, whole, anyspec, anyspec],
        out_specs=whole,
        scratch_shapes=[gathered, gathered,
                        pltpu.SemaphoreType.DMA((2,)),
                        pltpu.SemaphoreType.DMA((2,))])
    return pl.pallas_call(
        _sample_attn_kernel,
        grid_spec=grid_spec,
        out_shape=jax.ShapeDtypeStruct((n_seq, N_HEADS, HEAD_DIM), F32),
        compiler_params=pltpu.CompilerParams(dimension_semantics=("arbitrary",),
                                             vmem_limit_bytes=VMEM_LIMIT),
        name="attn_sample",
    )(page_table, ksum, q, k_new, v_new, cache_k, cache_v)


def kernel(x_prompt, x_sample, cache_k, cache_v, state_h, state_conv, page_table, w_in, conv_w, conv_b,
           w_rg_a, b_rg_a, w_rg_x, b_rg_x, lru_lambda, w_br_rnn, w_br_attn, w_out, ln1_g, ln1_b,
           w_up, b_up, w_down, b_down, ln2_g, ln2_b):
    assert w_in.shape[0] == 1, "single-layer trunk"
    b, t, _ = x_prompt.shape
    db = x_sample.shape[0]
    assert x_sample.shape[1] == 1 and t % MOBA_BLOCK == 0

    w_in_b = w_in[0].astype(BF16)
    w_gate = jnp.concatenate([w_rg_a[0], w_rg_x[0]], axis=-1).astype(BF16)
    cw, cb = conv_w[0], conv_b
    post_w = (w_br_rnn[0].astype(BF16), w_br_attn[0].astype(BF16), w_out[0].astype(BF16),
              w_up[0].astype(BF16), w_down[0].astype(BF16), ln1_g, ln1_b, b_up, b_down, ln2_g, ln2_b)

    (rnn_p, qt_p, k_p, v_p, kb_p, vt_p, sga_p, sgb_p, selt_p, h_p, cs_p, ksum_a) = _front_prompt(
        x_prompt, w_in_b, cw, cb, w_gate, b_rg_a, b_rg_x, lru_lambda, page_table, cache_k)
    o_p = _attn_prompt(qt_p, kb_p, vt_p, selt_p)
    flat = lambda a: a.reshape(b * t, D_MODEL)
    y_p, ksum_b = _post(flat(rnn_p), o_p, flat(sga_p), flat(sgb_p), flat(x_prompt), *post_w, tm=256,
                        stream=(page_table, cache_k, ksum_a.shape[0] * PAGES_PER_BLOCK))
    ksum = jnp.concatenate([ksum_a, ksum_b], axis=0)

    xs = x_sample.reshape(db, D_MODEL)
    sc = state_conv[0]
    (rnn_s, q_s, k_s, v_s, sga_s, sgb_s, h_s, u_s) = _front_sample(
        xs, sc[:, 0], sc[:, 1], sc[:, 2], state_h[0], w_in_b, cw, cb, w_gate, b_rg_a, b_rg_x, lru_lambda)
    heads = lambda a: a.reshape(db, N_HEADS, HEAD_DIM)
    o_s = _sample_attn(page_table, ksum, heads(q_s), heads(k_s), heads(v_s), cache_k, cache_v)
    o_s_hm = jnp.transpose(o_s, (1, 0, 2)).astype(BF16)[None]
    y_s = _post(rnn_s, o_s_hm, sga_s, sgb_s, xs, *post_w, tm=db)
    cs_s = jnp.concatenate([sc[:, 1:], u_s[:, None, :]], axis=1)

    kv_p = lambda a: a.reshape(1, b, t, N_HEADS, HEAD_DIM)
    kv_s = lambda a: a.reshape(1, db, 1, N_HEADS, HEAD_DIM)
    return (y_p.reshape(b, t, D_MODEL), y_s.reshape(db, 1, D_MODEL), kv_p(k_p), kv_p(v_p),
            h_p.reshape(1, b, D_MODEL), cs_p[None],
            kv_s(k_s), kv_s(v_s), h_s[None], cs_s[None])
```

```python
import functools
import math

import jax
import jax.numpy as jnp
from jax import lax
from jax.experimental import pallas as pl
from jax.experimental.pallas import tpu as pltpu

F32 = jnp.float32
BF16 = jnp.bfloat16

D_MODEL = 1024
N_HEADS = 8
HEAD_DIM = 128
RNN_BLOCKS = 8
RNN_BW = 128
CONV_W = 4
LRU_C = 8.0
MOBA_BLOCK = 256
MOBA_TOPK = 3
PAGE_SIZE = 128
PAGES_PER_BLOCK = MOBA_BLOCK // PAGE_SIZE
D_FF = 4096
ALPHA = 2.0 ** 0.25
LN_EPS = 1e-5
NEG = -1e30
SCALE = HEAD_DIM ** -0.5
LOG2E = math.log2(math.e)

SUBLANES = 8
VMEM_LIMIT = 56 * 1024 * 1024
HEADS_PER_ITER = 4


def _sigmoid(x):
    return 0.5 * jnp.tanh(0.5 * x) + 0.5


def _softplus(x):
    return jnp.maximum(x, 0.0) + jnp.log1p(jnp.exp(-jnp.abs(x)))


def _gelu_tanh(x):
    c = math.sqrt(2.0 / math.pi)
    return x * (0.5 * (1.0 + jnp.tanh(c * (x + 0.044715 * (x * x * x)))))


def _layer_norm(x, g, b):
    mu = jnp.mean(x, axis=-1, keepdims=True)
    xc = x - mu
    var = jnp.mean(xc * xc, axis=-1, keepdims=True)
    return xc * lax.rsqrt(var + LN_EPS) * g + b


def _split_bf16(x):
    hi = x.astype(BF16)
    lo = (x - hi.astype(F32)).astype(BF16)
    return hi, lo


def _rglru_gate_dot(ucb_blk, wg):
    return jnp.dot(ucb_blk, wg, preferred_element_type=F32)


def _rglru_block(uc_blk, ucb_blk, wg, ba, bx, neg_c_sp):
    return _rglru_elem(_rglru_gate_dot(ucb_blk, wg), uc_blk, ba, bx, neg_c_sp)


def _rglru_elem(gz, uc_blk, ba, bx, neg_c_sp):
    r = _sigmoid(gz[:, :RNN_BW] + ba)
    ig = _sigmoid(gz[:, RNN_BW:] + bx)
    log_a = (LRU_C * r) * neg_c_sp
    a = jnp.exp(log_a)
    t = jnp.tanh(log_a)
    mult = jnp.sqrt(-2.0 * t / (1.0 - t))
    return a, mult * (ig * uc_blk)


def _page_sum(page):
    rows = page.shape[0]
    parts = jnp.sum(page.reshape(SUBLANES, rows // SUBLANES, N_HEADS, HEAD_DIM), axis=1)
    return jnp.sum(parts, axis=0)


class _PageStream:
    def __init__(self, pt_ref, ck_ref, ring, sems, ksum_ref, step, n_steps, page_base, n_chunks):
        self.pt_ref, self.ck_ref, self.ring, self.sems, self.ksum_ref = pt_ref, ck_ref, ring, sems, ksum_ref
        self.step, self.n_steps, self.page_base, self.n_chunks = step, n_steps, page_base, n_chunks
        self.n_ring = ring.shape[0]
        self.half = self.n_ring // 2
        self.per_step = ksum_ref.shape[0] * PAGES_PER_BLOCK
        assert self.per_step == n_chunks * self.half and n_chunks % 2 == 0
        assert self.half % PAGES_PER_BLOCK == 0
        self.span = self.per_step * n_steps

    def _copy(self, rel, half_idx, s):
        n_pages = self.pt_ref.shape[1]
        g = self.page_base + lax.rem(jnp.asarray(rel, jnp.int32), jnp.asarray(self.span, jnp.int32))
        page = self.pt_ref[g // n_pages, lax.rem(g, n_pages)]
        return pltpu.make_async_copy(self.ck_ref.at[0, page], self.ring.at[half_idx * self.half + s],
                                     self.sems.at[half_idx])

    @staticmethod
    def _half_of(c):
        return c % 2 if isinstance(c, int) else lax.rem(c, 2)

    def prime(self):
        @pl.when(self.step == 0)
        def _():
            for s in range(self.n_ring):
                self._copy(s, s // self.half, s % self.half).start()

    def begin(self, c):
        for s in range(self.half):
            self._copy(0, self._half_of(c), s).wait()

    def sums(self, c, part, n_parts):
        blocks = self.half // PAGES_PER_BLOCK
        base = self._half_of(c) * self.half
        for blk in range(part * blocks // n_parts, (part + 1) * blocks // n_parts):
            acc = _page_sum(self.ring[base + blk * PAGES_PER_BLOCK])
            for t in range(1, PAGES_PER_BLOCK):
                acc = acc + _page_sum(self.ring[base + blk * PAGES_PER_BLOCK + t])
            self.ksum_ref[c * blocks + blk] = acc

    def end(self, c):
        for s in range(self.half):
            rel = self.step * self.per_step + c * self.half + s + self.n_ring
            self._copy(rel, self._half_of(c), s).start()

    def drain(self):
        @pl.when(self.step == self.n_steps - 1)
        def _():
            for s in range(self.n_ring):
                self._copy(0, s // self.half, s % self.half).wait()


class _NoStream:
    def prime(self): pass
    def begin(self, c): pass
    def sums(self, c, part, n_parts): pass
    def end(self, c): pass
    def drain(self): pass


STREAM_CHUNKS = 4
FRONT_RING = 8
POST_RING = 12
ATTN_RING = 24


def _front_kernel(pt_ref, x_ref, ck_ref, w_ref, cw_ref, cb_ref, wg_ref, ba_ref, bx_ref, lam_ref,
                  rnn_ref, qt_ref, k_ref, v_ref, kb_ref, vt_ref, sga_ref, sgb_ref, selt_ref,
                  h_ref, cs_ref, ksum_ref,
                  ubuf, abuf, sbuf, hbuf, hcar, kmt, ring, sems):
    i = pl.program_id(1)
    tt = x_ref.shape[0]
    n_blk = kmt.shape[0] // N_HEADS
    step = pl.program_id(0) * pl.num_programs(1) + i
    stream = _PageStream(pt_ref, ck_ref, ring, sems, ksum_ref, step,
                         pl.num_programs(0) * pl.num_programs(1), 0, STREAM_CHUNKS)
    stream.prime()
    stream.begin(0)

    @pl.when(i == 0)
    def _():
        ubuf[0:SUBLANES, :] = jnp.zeros((SUBLANES, D_MODEL), F32)
        hcar[...] = jnp.zeros(hcar.shape, F32)
        kmt[...] = jnp.zeros(kmt.shape, F32)

    xb = x_ref[...].astype(BF16)

    def proj(j):
        return jnp.dot(xb, w_ref[:, j * D_MODEL:(j + 1) * D_MODEL], preferred_element_type=F32)


    def gate_dots(ucb, lo, hi):
        return [_rglru_gate_dot(ucb[:, n * RNN_BW:(n + 1) * RNN_BW], wg_ref[n]) for n in range(lo, hi)]

    def rglru_elem(gzs, uc, neg_c_sp, lo):
        for k, gz in enumerate(gzs):
            sl = slice((lo + k) * RNN_BW, (lo + k + 1) * RNN_BW)
            a, uin = _rglru_elem(gz, uc[:, sl], ba_ref[:, sl], bx_ref[:, sl], neg_c_sp[:, sl])
            abuf[:, sl] = a
            sbuf[:, sl] = uin

    def scan_groups(h, lo, hi):
        sub = lax.broadcasted_iota(jnp.int32, (SUBLANES, D_MODEL), 0)
        for r in range(lo, hi):
            rows = slice(r * SUBLANES, (r + 1) * SUBLANES)
            a = abuf[rows, :]
            s = sbuf[rows, :]
            for sh in (1, 2, 4):
                a_prev = jnp.where(sub >= sh, pltpu.roll(a, sh, 0), 1.0)
                s_prev = jnp.where(sub >= sh, pltpu.roll(s, sh, 0), 0.0)
                s = s + a * s_prev
                a = a * a_prev
            hs_r = s + a * h
            hbuf[rows, :] = hs_r
            h = hs_r[SUBLANES - 1:SUBLANES, :]
        return h

    def query_side(qf):
        qt = qf.T
        qt_ref[...] = (qt * (SCALE * LOG2E)).astype(BF16)
        q_hi, q_lo = _split_bf16(qt)
        km_hi, km_lo = _split_bf16(kmt[...])
        nk = kmt.shape[0]
        g2 = jnp.dot(jnp.concatenate([km_hi, km_lo], axis=0), q_hi, preferred_element_type=F32)
        gate = g2[:nk] + g2[nk:] + jnp.dot(km_hi, q_lo, preferred_element_type=F32)
        gn = [gate[n * N_HEADS:(n + 1) * N_HEADS, :] for n in range(n_blk)]
        for n in range(n_blk):
            cnt = jnp.zeros(gn[n].shape, jnp.int32)
            for m in range(n_blk):
                if m == n:
                    continue
                beats = (gn[m] >= gn[n]) if m < n else (gn[m] > gn[n])
                cnt = cnt + jnp.where(beats, 1, 0) * jnp.where(m < i, 1, 0)
            keep = jnp.where(cnt < MOBA_TOPK, 1.0, 0.0) * jnp.where(n < i, 1.0, 0.0)
            selt_ref[n * N_HEADS:(n + 1) * N_HEADS, :] = keep.astype(F32)

    def key_side(kf):
        k_ref[...] = kf
        for hd in range(N_HEADS):
            kb_ref[hd] = kf[:, hd * HEAD_DIM:(hd + 1) * HEAD_DIM].astype(BF16)
        kmean = jnp.sum(kf, axis=0, keepdims=True) * (1.0 / MOBA_BLOCK)
        head_of_lane = lax.broadcasted_iota(jnp.int32, (N_HEADS, D_MODEL), 1) // HEAD_DIM
        row = lax.broadcasted_iota(jnp.int32, (N_HEADS, D_MODEL), 0)
        kmt[pl.ds(pl.multiple_of(i * N_HEADS, N_HEADS), N_HEADS), :] = jnp.where(
            head_of_lane == row, jnp.broadcast_to(kmean, (N_HEADS, D_MODEL)), 0.0)

    u = proj(0)
    qf = proj(2)
    stream.sums(0, 0, 2)
    ubuf[SUBLANES:SUBLANES + tt, :] = u
    uc = cb_ref[...] + cw_ref[CONV_W - 1:CONV_W, :] * u
    for j in range(CONV_W - 1):
        s = SUBLANES - (CONV_W - 1) + j
        uc = uc + cw_ref[j:j + 1, :] * ubuf[s:s + tt, :]
    cs_ref[...] = ubuf[SUBLANES + tt - (CONV_W - 1):SUBLANES + tt, :]
    ubuf[0:SUBLANES, :] = ubuf[tt:tt + SUBLANES, :]
    neg_c_sp = -_softplus(-lam_ref[...])
    ucb = uc.astype(BF16)
    kf = proj(3)
    stream.sums(0, 1, 2)
    stream.end(0)

    stream.begin(1)
    query_side(qf)
    gz = gate_dots(ucb, 0, 2)
    vf = proj(4)
    stream.sums(1, 0, 2)
    key_side(kf)
    rglru_elem(gz, uc, neg_c_sp, 0)
    stream.sums(1, 1, 2)
    stream.end(1)

    stream.begin(2)
    gz = gate_dots(ucb, 2, 4)
    pre_a = proj(5)
    stream.sums(2, 0, 2)
    v_ref[...] = vf
    vt_ref[...] = vf.T.astype(BF16)
    rglru_elem(gz, uc, neg_c_sp, 2)
    gz = gate_dots(ucb, 4, 6)
    pre_b = proj(6)
    stream.sums(2, 1, 2)
    sga_ref[...] = _sigmoid(pre_a).astype(sga_ref.dtype)
    rglru_elem(gz, uc, neg_c_sp, 4)
    stream.end(2)

    stream.begin(3)
    gz = gate_dots(ucb, 6, RNN_BLOCKS)
    g = proj(1)
    stream.sums(3, 0, 2)
    sgb_ref[...] = _sigmoid(pre_b).astype(sgb_ref.dtype)
    rglru_elem(gz, uc, neg_c_sp, 6)
    stream.sums(3, 1, 2)
    h = scan_groups(hcar[0:1, :], 0, tt // SUBLANES)
    hcar[0:1, :] = h
    h_ref[...] = h
    rnn_ref[...] = (hbuf[...] * _gelu_tanh(g)).astype(BF16)
    stream.end(3)
    stream.drain()


def _const_spec(shape):
    nd = len(shape)
    return pl.BlockSpec(shape, lambda *_: (0,) * nd, pipeline_mode=pl.Buffered(1))


def _front_prompt(x, w_in, conv_w, conv_b, w_gate, b_a, b_x, lam, page_table, cache_k):
    b, t, _ = x.shape
    tt = MOBA_BLOCK
    nt = t // tt
    blocks_per_step = STREAM_CHUNKS * (FRONT_RING // 2) // PAGES_PER_BLOCK
    tile = pl.BlockSpec((None, tt, D_MODEL), lambda bi, ti, *_: (bi, ti, 0))
    ttile = pl.BlockSpec((None, D_MODEL, tt), lambda bi, ti, *_: (bi, 0, ti))
    big = lambda dt: jax.ShapeDtypeStruct((b, t, D_MODEL), dt)
    tbig = jax.ShapeDtypeStruct((b, D_MODEL, t), BF16)
    hm_tile = pl.BlockSpec((None, N_HEADS, tt, HEAD_DIM), lambda bi, ti, *_: (bi, 0, ti, 0))
    hm_big = jax.ShapeDtypeStruct((b, N_HEADS, t, HEAD_DIM), BF16)
    out_shape = (big(BF16), tbig, big(F32), big(F32), hm_big, tbig, big(BF16), big(BF16),
                 jax.ShapeDtypeStruct((b, nt, nt * N_HEADS, tt), F32),
                 jax.ShapeDtypeStruct((b, 1, D_MODEL), F32),
                 jax.ShapeDtypeStruct((b, CONV_W - 1, D_MODEL), F32),
                 jax.ShapeDtypeStruct((b * nt * blocks_per_step, N_HEADS, HEAD_DIM), F32))
    out_specs = (tile, ttile, tile, tile, hm_tile, ttile, tile, tile,
                 pl.BlockSpec((None, None, nt * N_HEADS, tt), lambda bi, ti, *_: (bi, ti, 0, 0)),
                 pl.BlockSpec((None, 1, D_MODEL), lambda bi, ti, *_: (bi, 0, 0)),
                 pl.BlockSpec((None, CONV_W - 1, D_MODEL), lambda bi, ti, *_: (bi, 0, 0)),
                 pl.BlockSpec((blocks_per_step, N_HEADS, HEAD_DIM), lambda bi, ti, *_: (bi * nt + ti, 0, 0)))
    in_specs = [tile, pl.BlockSpec(memory_space=pl.ANY),
                _const_spec(w_in.shape), _const_spec(conv_w.shape), _const_spec(conv_b.shape),
                _const_spec(w_gate.shape), _const_spec(b_a.shape), _const_spec(b_x.shape), _const_spec(lam.shape)]
    grid_spec = pltpu.PrefetchScalarGridSpec(
        num_scalar_prefetch=1,
        grid=(b, nt),
        in_specs=in_specs,
        out_specs=out_specs,
        scratch_shapes=[pltpu.VMEM((SUBLANES + tt, D_MODEL), F32),
                        pltpu.VMEM((tt, D_MODEL), F32),
                        pltpu.VMEM((tt, D_MODEL), F32),
                        pltpu.VMEM((tt, D_MODEL), F32),
                        pltpu.VMEM((SUBLANES, D_MODEL), F32),
                        pltpu.VMEM((nt * N_HEADS, D_MODEL), F32),
                        pltpu.VMEM((FRONT_RING, PAGE_SIZE, N_HEADS, HEAD_DIM), F32),
                        pltpu.SemaphoreType.DMA((2,))])
    return pl.pallas_call(
        _front_kernel,
        grid_spec=grid_spec,
        out_shape=out_shape,
        compiler_params=pltpu.CompilerParams(dimension_semantics=("arbitrary", "arbitrary"),
                                             vmem_limit_bytes=VMEM_LIMIT),
        name="front_prompt",
    )(page_table, x, cache_k, w_in, conv_w, conv_b, w_gate, b_a, b_x, lam)


def _attn_kernel(page_base, pt_ref, qt_ref, kb_ref, vt_ref, selt_ref, ck_ref, o_ref, ksum_ref, ring, sems):
    i = pl.program_id(1)
    tq = qt_ref.shape[1]
    n_q = kb_ref.shape[1] // MOBA_BLOCK
    n_groups = N_HEADS // HEADS_PER_ITER
    stream = _PageStream(pt_ref, ck_ref, ring, sems, ksum_ref, pl.program_id(0) * pl.num_programs(1) + i,
                         pl.num_programs(0) * pl.num_programs(1), page_base, n_groups)
    stream.prime()
    key_idx = lax.broadcasted_iota(jnp.int32, (MOBA_BLOCK, tq), 0)
    qry_idx = lax.broadcasted_iota(jnp.int32, (MOBA_BLOCK, tq), 1)
    causal = key_idx <= qry_idx

    def head_rows(h):
        return pl.ds(pl.multiple_of(h * HEAD_DIM, HEAD_DIM), HEAD_DIM)

    def masked_scores(h, n_past):
        nk = (n_past + 1) * MOBA_BLOCK
        s = jnp.dot(kb_ref[h, 0:nk, :], qt_ref[head_rows(h), :], preferred_element_type=F32)
        parts = []
        for j in range(n_past):
            keep = selt_ref[pl.ds(j * N_HEADS + h, 1), :]
            parts.append(jnp.where(keep > 0.5, s[j * MOBA_BLOCK:(j + 1) * MOBA_BLOCK, :], NEG))
        parts.append(jnp.where(causal, s[n_past * MOBA_BLOCK:, :], NEG))
        return jnp.concatenate(parts, axis=0) if n_past else parts[0]

    def softmax_pv(h, s):
        nk = s.shape[0]
        m = jnp.max(s, axis=0, keepdims=True)
        p = jnp.exp2(s - m)
        l = jnp.sum(p, axis=0, keepdims=True)
        ot = jnp.dot(vt_ref[head_rows(h), 0:nk], p.astype(BF16), preferred_element_type=F32) * (1.0 / l)
        o_ref[h] = ot.T.astype(BF16)

    def attend(n_past):
        def head_group(g, carry):
            heads = [g * HEADS_PER_ITER + u for u in range(HEADS_PER_ITER)]
            stream.begin(g)
            scores = [masked_scores(h, n_past) for h in heads]
            for k, (h, s) in enumerate(zip(heads, scores)):
                stream.sums(g, k, HEADS_PER_ITER)
                softmax_pv(h, s)
            stream.end(g)
            return carry
        lax.fori_loop(0, n_groups, head_group, 0)

    for v in range(n_q):
        @pl.when(i == v)
        def _(v=v):
            attend(v)
    stream.drain()


def _attn_prompt(qt, kb, vt, selt, page_table, cache_k, page_base):
    b, _, t = qt.shape
    tq = MOBA_BLOCK
    nq = t // tq
    blocks_per_step = (N_HEADS // HEADS_PER_ITER) * (ATTN_RING // 2) // PAGES_PER_BLOCK
    grid_spec = pltpu.PrefetchScalarGridSpec(
        num_scalar_prefetch=1,
        grid=(b, nq),
        in_specs=[pl.BlockSpec((None, D_MODEL, tq), lambda bi, qi, *_: (bi, 0, qi)),
                  pl.BlockSpec((None, N_HEADS, t, HEAD_DIM), lambda bi, qi, *_: (bi, 0, 0, 0)),
                  pl.BlockSpec((None, D_MODEL, t), lambda bi, qi, *_: (bi, 0, 0)),
                  pl.BlockSpec((None, None, nq * N_HEADS, tq), lambda bi, qi, *_: (bi, qi, 0, 0)),
                  pl.BlockSpec(memory_space=pl.ANY)],
        out_specs=(pl.BlockSpec((None, N_HEADS, tq, HEAD_DIM), lambda bi, qi, *_: (bi, 0, qi, 0)),
                   pl.BlockSpec((blocks_per_step, N_HEADS, HEAD_DIM), lambda bi, qi, *_: (bi * nq + qi, 0, 0))),
        scratch_shapes=[pltpu.VMEM((ATTN_RING, PAGE_SIZE, N_HEADS, HEAD_DIM), F32),
                        pltpu.SemaphoreType.DMA((2,))])
    return pl.pallas_call(
        functools.partial(_attn_kernel, page_base),
        grid_spec=grid_spec,
        out_shape=(jax.ShapeDtypeStruct((b, N_HEADS, t, HEAD_DIM), BF16),
                   jax.ShapeDtypeStruct((b * nq * blocks_per_step, N_HEADS, HEAD_DIM), F32)),
        compiler_params=pltpu.CompilerParams(dimension_semantics=("arbitrary", "arbitrary"),
                                             vmem_limit_bytes=VMEM_LIMIT),
        name="attn_prompt",
    )(page_table, qt, kb, vt, selt, cache_k)


def _post_body(rnn_ref, o_ref, sga_ref, sgb_ref, x_ref, wbr_ref, wba_ref, wout_ref, wup_ref, wdn_ref,
               g1_ref, b1_ref, bup_ref, bdn_ref, g2_ref, b2_ref, y_ref, stream):
    stream.begin(0)
    y_rnn = jnp.dot(rnn_ref[...], wbr_ref[...], preferred_element_type=F32)
    stream.sums(0, 0, 2)
    o = jnp.concatenate([o_ref[h] for h in range(N_HEADS)], axis=-1)
    y_att = jnp.dot(o, wba_ref[...], preferred_element_type=F32)
    stream.sums(0, 1, 2)
    mix = sga_ref[...] * y_rnn + sgb_ref[...] * y_att
    stream.end(0)
    stream.begin(1)
    t1 = ALPHA * x_ref[...] + jnp.dot(mix.astype(BF16), wout_ref[...], preferred_element_type=F32)
    stream.sums(1, 0, 2)
    x1 = _layer_norm(t1, g1_ref[...], b1_ref[...])
    x1b = x1.astype(BF16)
    acc = ALPHA * x1 + bdn_ref[...]
    stream.sums(1, 1, 2)
    stream.end(1)
    n_chunks = D_FF // D_MODEL
    for c in range(n_chunks):
        stage, part = 2 + c // 2, c % 2
        if part == 0:
            stream.begin(stage)
        cs = slice(c * D_MODEL, (c + 1) * D_MODEL)
        hid = jnp.maximum(jnp.dot(x1b, wup_ref[:, cs], preferred_element_type=F32) + bup_ref[:, cs], 0.0)
        stream.sums(stage, 2 * part, 4)
        acc = acc + jnp.dot((hid * hid).astype(BF16), wdn_ref[cs, :], preferred_element_type=F32)
        stream.sums(stage, 2 * part + 1, 4)
        if part == 1 and stage == 2:
            stream.end(stage)
    y_ref[...] = _layer_norm(acc, g2_ref[...], b2_ref[...])
    stream.end(3)


def _post_kernel(*refs):
    _post_body(*refs, stream=_NoStream())


def _post_stream_kernel(page_base, pt_ref, rnn_ref, o_ref, sga_ref, sgb_ref, x_ref, ck_ref, *rest):
    consts, (y_ref, ksum_ref, ring, sems) = rest[:11], rest[11:]
    stream = _PageStream(pt_ref, ck_ref, ring, sems, ksum_ref, pl.program_id(0), pl.num_programs(0),
                         page_base, STREAM_CHUNKS)
    stream.prime()
    _post_body(rnn_ref, o_ref, sga_ref, sgb_ref, x_ref, *consts, y_ref, stream=stream)
    stream.drain()


def _post(rnn, o, sga, sgb, x, wbr, wba, wout, wup, wdn, g1, b1, bup, bdn, g2, b2, tm, stream=None):
    m = x.shape[0]
    n_steps = m // tm
    tiles_per_seq = o.shape[2] // tm
    assert o.shape[0] * o.shape[2] == m
    tile = pl.BlockSpec((tm, D_MODEL), lambda r, *_: (r, 0))
    o_tile = pl.BlockSpec((None, N_HEADS, tm, HEAD_DIM),
                          lambda r, *_: (r // tiles_per_seq, 0, r % tiles_per_seq, 0))
    consts = (wbr, wba, wout, wup, wdn, g1, b1, bup, bdn, g2, b2)
    params = pltpu.CompilerParams(dimension_semantics=("arbitrary",), vmem_limit_bytes=VMEM_LIMIT)
    y_shape = jax.ShapeDtypeStruct((m, D_MODEL), F32)
    if stream is None:
        return pl.pallas_call(
            _post_kernel,
            grid=(n_steps,),
            in_specs=[tile, o_tile, tile, tile, tile] + [_const_spec(c.shape) for c in consts],
            out_specs=tile,
            out_shape=y_shape,
            compiler_params=params,
            name="post_m%d" % m,
        )(rnn, o, sga, sgb, x, *consts)
    page_table, cache_k, page_base = stream
    blocks_per_step = STREAM_CHUNKS * (POST_RING // 2) // PAGES_PER_BLOCK
    n_blocks = blocks_per_step * n_steps
    assert page_base + n_blocks * PAGES_PER_BLOCK == page_table.size, "front + post shares must cover all pages"
    grid_spec = pltpu.PrefetchScalarGridSpec(
        num_scalar_prefetch=1,
        grid=(n_steps,),
        in_specs=([tile, o_tile, tile, tile, tile, pl.BlockSpec(memory_space=pl.ANY)]
                  + [_const_spec(c.shape) for c in consts]),
        out_specs=(tile, pl.BlockSpec((blocks_per_step, N_HEADS, HEAD_DIM), lambda r, *_: (r, 0, 0))),
        scratch_shapes=[pltpu.VMEM((POST_RING, PAGE_SIZE, N_HEADS, HEAD_DIM), F32),
                        pltpu.SemaphoreType.DMA((2,))])
    return pl.pallas_call(
        functools.partial(_post_stream_kernel, page_base),
        grid_spec=grid_spec,
        out_shape=(y_shape, jax.ShapeDtypeStruct((n_blocks, N_HEADS, HEAD_DIM), F32)),
        compiler_params=params,
        name="post_stream_m%d" % m,
    )(page_table, rnn, o, sga, sgb, x, cache_k, *consts)


def _front_sample_kernel(x_ref, p0_ref, p1_ref, p2_ref, h0_ref, w_ref, cw_ref, cb_ref, wg_ref, ba_ref,
                         bx_ref, lam_ref,
                         rnn_ref, q_ref, k_ref, v_ref, sga_ref, sgb_ref, h_ref, u_ref):
    xb = x_ref[...].astype(BF16)

    def proj(j):
        return jnp.dot(xb, w_ref[:, j * D_MODEL:(j + 1) * D_MODEL], preferred_element_type=F32)

    u = proj(0)
    u_ref[...] = u
    uc = (cb_ref[...] + cw_ref[0:1, :] * p0_ref[...] + cw_ref[1:2, :] * p1_ref[...]
          + cw_ref[2:3, :] * p2_ref[...] + cw_ref[3:4, :] * u)
    neg_c_sp = -_softplus(-lam_ref[...])
    ucb = uc.astype(BF16)
    g = proj(1)
    for n in range(RNN_BLOCKS):
        sl = slice(n * RNN_BW, (n + 1) * RNN_BW)
        a, uin = _rglru_block(uc[:, sl], ucb[:, sl], wg_ref[n], ba_ref[:, sl], bx_ref[:, sl],
                              neg_c_sp[:, sl])
        h = a * h0_ref[:, sl] + uin
        h_ref[:, sl] = h
        rnn_ref[:, sl] = (h * _gelu_tanh(g[:, sl])).astype(BF16)
    q_ref[...] = proj(2)
    k_ref[...] = proj(3)
    v_ref[...] = proj(4)
    sga_ref[...] = _sigmoid(proj(5))
    sgb_ref[...] = _sigmoid(proj(6))


def _front_sample(x, p0, p1, p2, h0, w_in, conv_w, conv_b, w_gate, b_a, b_x, lam):
    m = x.shape[0]
    args = (x, p0, p1, p2, h0, w_in, conv_w, conv_b, w_gate, b_a, b_x, lam)
    row = lambda dt: jax.ShapeDtypeStruct((m, D_MODEL), dt)
    return pl.pallas_call(
        _front_sample_kernel,
        grid=(1,),
        in_specs=[_const_spec(a.shape) for a in args],
        out_specs=tuple(pl.BlockSpec((m, D_MODEL), lambda r: (0, 0)) for _ in range(8)),
        out_shape=(row(BF16), row(F32), row(F32), row(F32), row(F32), row(F32), row(F32), row(F32)),
        compiler_params=pltpu.CompilerParams(dimension_semantics=("arbitrary",),
                                             vmem_limit_bytes=VMEM_LIMIT),
        name="front_sample",
    )(*args)


def _sample_attn_kernel(pt_ref, ksum_ref, q_ref, kn_ref, vn_ref, ck_ref, cv_ref, o_ref,
                        kbuf, vbuf, ksem, vsem):
    b = pl.program_id(0)
    n_seq = pl.num_programs(0)
    n_blocks = ksum_ref.shape[0]
    n_gather = MOBA_TOPK * PAGES_PER_BLOCK
    par = lax.rem(b, 2)

    gate = jnp.sum(ksum_ref[...] * (1.0 / MOBA_BLOCK) * q_ref[b][None], axis=-1, keepdims=True)
    blk = lax.broadcasted_iota(jnp.int32, gate.shape, 0)

    def gather_copy(src_ref, dst_ref, sem, page, h, slot, buf):
        return pltpu.make_async_copy(src_ref.at[0, page, :, h, :], dst_ref.at[buf, h, slot], sem.at[buf])

    for t in range(MOBA_TOPK):
        mx = jnp.max(gate, axis=0, keepdims=True)
        idx = jnp.min(jnp.where(gate == mx, blk, n_blocks), axis=0, keepdims=True)
        gate = jnp.where(blk == idx, -jnp.inf, gate)
        for h in range(N_HEADS):
            sel = idx[0, h, 0]
            for pg in range(PAGES_PER_BLOCK):
                page = pt_ref[b, sel * PAGES_PER_BLOCK + pg]
                slot = t * PAGES_PER_BLOCK + pg
                gather_copy(ck_ref, kbuf, ksem, page, h, slot, par).start()
                gather_copy(cv_ref, vbuf, vsem, page, h, slot, par).start()

    def attend(seq, buf):
        for h in range(N_HEADS):
            for slot in range(n_gather):
                gather_copy(ck_ref, kbuf, ksem, 0, h, slot, buf).wait()
                gather_copy(cv_ref, vbuf, vsem, 0, h, slot, buf).wait()
        q = q_ref[seq]
        kn = kn_ref[seq]
        vn = vn_ref[seq]
        for h in range(N_HEADS):
            qh = q[h:h + 1, :]
            kh = kbuf[buf, h].reshape(n_gather * PAGE_SIZE, HEAD_DIM)
            vh = vbuf[buf, h].reshape(n_gather * PAGE_SIZE, HEAD_DIM)
            s = jnp.sum(kh * qh, axis=-1, keepdims=True) * SCALE
            s_new = jnp.sum(kn[h:h + 1, :] * qh, axis=-1, keepdims=True) * SCALE
            m = jnp.maximum(jnp.max(s, axis=0, keepdims=True), s_new)
            p = jnp.exp(s - m)
            p_new = jnp.exp(s_new - m)
            l = jnp.sum(p, axis=0, keepdims=True) + p_new
            o = jnp.sum(p * vh, axis=0, keepdims=True) + p_new * vn[h:h + 1, :]
            o_ref[seq, h:h + 1, :] = o / l

    @pl.when(b > 0)
    def _():
        attend(b - 1, 1 - par)

    @pl.when(b == n_seq - 1)
    def _():
        attend(b, par)


def _sample_attn(page_table, ksum, q, k_new, v_new, cache_k, cache_v):
    n_seq, n_pages = page_table.shape
    assert (n_pages * PAGE_SIZE) % MOBA_BLOCK == 0, "own-block cached prefix is not supported"
    n_blocks = n_pages // PAGES_PER_BLOCK
    assert n_blocks >= MOBA_TOPK and ksum.shape[0] == n_seq * n_blocks
    n_gather = MOBA_TOPK * PAGES_PER_BLOCK
    whole = pl.BlockSpec((n_seq, N_HEADS, HEAD_DIM), lambda s, pt: (0, 0, 0))
    anyspec = pl.BlockSpec(memory_space=pl.ANY)
    gathered = pltpu.VMEM((2, N_HEADS, n_gather, PAGE_SIZE, HEAD_DIM), F32)
    grid_spec = pltpu.PrefetchScalarGridSpec(
        num_scalar_prefetch=1,
        grid=(n_seq,),
        in_specs=[pl.BlockSpec((n_blocks, N_HEADS, HEAD_DIM), lambda s, pt: (s, 0, 0)),
                  whole, whole, whole, anyspec, anyspec],
        out_specs=whole,
        scratch_shapes=[gathered, gathered,
                        pltpu.SemaphoreType.DMA((2,)),
                        pltpu.SemaphoreType.DMA((2,))])
    return pl.pallas_call(
        _sample_attn_kernel,
        grid_spec=grid_spec,
        out_shape=jax.ShapeDtypeStruct((n_seq, N_HEADS, HEAD_DIM), F32),
        compiler_params=pltpu.CompilerParams(dimension_semantics=("arbitrary",),
                                             vmem_limit_bytes=VMEM_LIMIT),
        name="attn_sample",
    )(page_table, ksum, q, k_new, v_new, cache_k, cache_v)


def kernel(x_prompt, x_sample, cache_k, cache_v, state_h, state_conv, page_table, w_in, conv_w, conv_b,
           w_rg_a, b_rg_a, w_rg_x, b_rg_x, lru_lambda, w_br_rnn, w_br_attn, w_out, ln1_g, ln1_b,
           w_up, b_up, w_down, b_down, ln2_g, ln2_b):
    assert w_in.shape[0] == 1, "single-layer trunk"
    b, t, _ = x_prompt.shape
    db = x_sample.shape[0]
    assert x_sample.shape[1] == 1 and t % MOBA_BLOCK == 0

    w_in_b = w_in[0].astype(BF16)
    w_gate = jnp.concatenate([w_rg_a[0], w_rg_x[0]], axis=-1).astype(BF16)
    cw, cb = conv_w[0], conv_b
    post_w = (w_br_rnn[0].astype(BF16), w_br_attn[0].astype(BF16), w_out[0].astype(BF16),
              w_up[0].astype(BF16), w_down[0].astype(BF16), ln1_g, ln1_b, b_up, b_down, ln2_g, ln2_b)

    (rnn_p, qt_p, k_p, v_p, kb_p, vt_p, sga_p, sgb_p, selt_p, h_p, cs_p, ksum_a) = _front_prompt(
        x_prompt, w_in_b, cw, cb, w_gate, b_rg_a, b_rg_x, lru_lambda, page_table, cache_k)
    pages_a = ksum_a.shape[0] * PAGES_PER_BLOCK
    o_p, ksum_b = _attn_prompt(qt_p, kb_p, vt_p, selt_p, page_table, cache_k, pages_a)
    pages_ab = pages_a + ksum_b.shape[0] * PAGES_PER_BLOCK
    flat = lambda a: a.reshape(b * t, D_MODEL)
    y_p, ksum_c = _post(flat(rnn_p), o_p, flat(sga_p), flat(sgb_p), flat(x_prompt), *post_w, tm=256,
                        stream=(page_table, cache_k, pages_ab))
    ksum = jnp.concatenate([ksum_a, ksum_b, ksum_c], axis=0)

    xs = x_sample.reshape(db, D_MODEL)
    sc = state_conv[0]
    (rnn_s, q_s, k_s, v_s, sga_s, sgb_s, h_s, u_s) = _front_sample(
        xs, sc[:, 0], sc[:, 1], sc[:, 2], state_h[0], w_in_b, cw, cb, w_gate, b_rg_a, b_rg_x, lru_lambda)
    heads = lambda a: a.reshape(db, N_HEADS, HEAD_DIM)
    o_s = _sample_attn(page_table, ksum, heads(q_s), heads(k_s), heads(v_s), cache_k, cache_v)
    o_s_hm = jnp.transpose(o_s, (1, 0, 2)).astype(BF16)[None]
    y_s = _post(rnn_s, o_s_hm, sga_s, sgb_s, xs, *post_w, tm=db)
    cs_s = jnp.concatenate([sc[:, 1:], u_s[:, None, :]], axis=1)

    kv_p = lambda a: a.reshape(1, b, t, N_HEADS, HEAD_DIM)
    kv_s = lambda a: a.reshape(1, db, 1, N_HEADS, HEAD_DIM)
    return (y_p.reshape(b, t, D_MODEL), y_s.reshape(db, 1, D_MODEL), kv_p(k_p), kv_p(v_p),
            h_p.reshape(1, b, D_MODEL), cs_p[None],
            kv_s(k_s), kv_s(v_s), h_s[None], cs_s[None])
```

```python
import functools
import math

import jax
import jax.numpy as jnp
from jax import lax
from jax.experimental import pallas as pl
from jax.experimental.pallas import tpu as pltpu

F32 = jnp.float32
BF16 = jnp.bfloat16

D_MODEL = 1024
N_HEADS = 8
HEAD_DIM = 128
RNN_BLOCKS = 8
RNN_BW = 128
CONV_W = 4
LRU_C = 8.0
MOBA_BLOCK = 256
MOBA_TOPK = 3
PAGE_SIZE = 128
PAGES_PER_BLOCK = MOBA_BLOCK // PAGE_SIZE
D_FF = 4096
ALPHA = 2.0 ** 0.25
LN_EPS = 1e-5
NEG = -1e30
SCALE = HEAD_DIM ** -0.5
LOG2E = math.log2(math.e)

SUBLANES = 8
VMEM_LIMIT = 56 * 1024 * 1024
HEADS_PER_ITER = 4


def _sigmoid(x):
    return 0.5 * jnp.tanh(0.5 * x) + 0.5


def _softplus(x):
    return jnp.maximum(x, 0.0) + jnp.log1p(jnp.exp(-jnp.abs(x)))


def _gelu_tanh(x):
    c = math.sqrt(2.0 / math.pi)
    return x * (0.5 * (1.0 + jnp.tanh(c * (x + 0.044715 * (x * x * x)))))


def _layer_norm(x, g, b):
    mu = jnp.mean(x, axis=-1, keepdims=True)
    xc = x - mu
    var = jnp.mean(xc * xc, axis=-1, keepdims=True)
    return xc * lax.rsqrt(var + LN_EPS) * g + b


def _split_bf16(x):
    hi = x.astype(BF16)
    lo = (x - hi.astype(F32)).astype(BF16)
    return hi, lo


def _rglru_gate_dot(ucb_blk, wg):
    return jnp.dot(ucb_blk, wg, preferred_element_type=F32)


def _rglru_block(uc_blk, ucb_blk, wg, ba, bx, neg_c_sp):
    return _rglru_elem(_rglru_gate_dot(ucb_blk, wg), uc_blk, ba, bx, neg_c_sp)


def _rglru_elem(gz, uc_blk, ba, bx, neg_c_sp):
    r = _sigmoid(gz[:, :RNN_BW] + ba)
    ig = _sigmoid(gz[:, RNN_BW:] + bx)
    log_a = (LRU_C * r) * neg_c_sp
    a = jnp.exp(log_a)
    t = jnp.tanh(log_a)
    mult = jnp.sqrt(-2.0 * t / (1.0 - t))
    return a, mult * (ig * uc_blk)


def _page_sum(page):
    rows = page.shape[0]
    parts = jnp.sum(page.reshape(SUBLANES, rows // SUBLANES, N_HEADS, HEAD_DIM), axis=1)
    return jnp.sum(parts, axis=0)


class _PageStream:
    def __init__(self, pt_ref, ck_ref, ring, sems, ksum_ref, step, n_steps, page_base, n_chunks):
        self.pt_ref, self.ck_ref, self.ring, self.sems, self.ksum_ref = pt_ref, ck_ref, ring, sems, ksum_ref
        self.step, self.n_steps, self.page_base, self.n_chunks = step, n_steps, page_base, n_chunks
        self.n_ring = ring.shape[0]
        self.half = self.n_ring // 2
        self.per_step = ksum_ref.shape[0] * PAGES_PER_BLOCK
        assert self.per_step == n_chunks * self.half and n_chunks % 2 == 0
        assert self.half % PAGES_PER_BLOCK == 0
        self.span = self.per_step * n_steps

    def _copy(self, rel, half_idx, s):
        n_pages = self.pt_ref.shape[1]
        g = self.page_base + lax.rem(jnp.asarray(rel, jnp.int32), jnp.asarray(self.span, jnp.int32))
        page = self.pt_ref[g // n_pages, lax.rem(g, n_pages)]
        return pltpu.make_async_copy(self.ck_ref.at[0, page], self.ring.at[half_idx * self.half + s],
                                     self.sems.at[half_idx])

    @staticmethod
    def _half_of(c):
        return c % 2 if isinstance(c, int) else lax.rem(c, 2)

    def prime(self):
        @pl.when(self.step == 0)
        def _():
            for s in range(self.n_ring):
                self._copy(s, s // self.half, s % self.half).start()

    def begin(self, c):
        for s in range(self.half):
            self._copy(0, self._half_of(c), s).wait()

    def sums(self, c, part, n_parts):
        blocks = self.half // PAGES_PER_BLOCK
        base = self._half_of(c) * self.half
        for blk in range(part * blocks // n_parts, (part + 1) * blocks // n_parts):
            acc = _page_sum(self.ring[base + blk * PAGES_PER_BLOCK])
            for t in range(1, PAGES_PER_BLOCK):
                acc = acc + _page_sum(self.ring[base + blk * PAGES_PER_BLOCK + t])
            self.ksum_ref[c * blocks + blk] = acc

    def end(self, c):
        for s in range(self.half):
            rel = self.step * self.per_step + c * self.half + s + self.n_ring
            self._copy(rel, self._half_of(c), s).start()

    def drain(self):
        @pl.when(self.step == self.n_steps - 1)
        def _():
            for s in range(self.n_ring):
                self._copy(0, s // self.half, s % self.half).wait()


class _NoStream:
    def prime(self): pass
    def begin(self, c): pass
    def sums(self, c, part, n_parts): pass
    def end(self, c): pass
    def drain(self): pass


STREAM_CHUNKS = 4
FRONT_RING = 8
POST_RING = 12
ATTN_RING = 24


def _front_kernel(pt_ref, x_ref, ck_ref, w_ref, cw_ref, cb_ref, wg_ref, ba_ref, bx_ref, lam_ref,
                  rnn_ref, qt_ref, k_ref, v_ref, kb_ref, vt_ref, sga_ref, sgb_ref, selt_ref,
                  h_ref, cs_ref, ksum_ref,
                  ubuf, abuf, sbuf, hbuf, hcar, kmt, ring, sems):
    i = pl.program_id(1)
    tt = x_ref.shape[0]
    n_blk = kmt.shape[0] // N_HEADS
    step = pl.program_id(0) * pl.num_programs(1) + i
    stream = _PageStream(pt_ref, ck_ref, ring, sems, ksum_ref, step,
                         pl.num_programs(0) * pl.num_programs(1), 0, STREAM_CHUNKS)
    stream.prime()
    stream.begin(0)

    @pl.when(i == 0)
    def _():
        ubuf[0:SUBLANES, :] = jnp.zeros((SUBLANES, D_MODEL), F32)
        hcar[...] = jnp.zeros(hcar.shape, F32)
        kmt[...] = jnp.zeros(kmt.shape, F32)

    xb = x_ref[...].astype(BF16)

    def proj(j):
        return jnp.dot(xb, w_ref[:, j * D_MODEL:(j + 1) * D_MODEL], preferred_element_type=F32)


    def gate_dots(ucb, lo, hi):
        return [_rglru_gate_dot(ucb[:, n * RNN_BW:(n + 1) * RNN_BW], wg_ref[n]) for n in range(lo, hi)]

    def rglru_elem(gzs, uc, neg_c_sp, lo):
        for k, gz in enumerate(gzs):
            sl = slice((lo + k) * RNN_BW, (lo + k + 1) * RNN_BW)
            a, uin = _rglru_elem(gz, uc[:, sl], ba_ref[:, sl], bx_ref[:, sl], neg_c_sp[:, sl])
            abuf[:, sl] = a
            sbuf[:, sl] = uin

    def scan_groups(h, lo, hi):
        sub = lax.broadcasted_iota(jnp.int32, (SUBLANES, D_MODEL), 0)
        for r in range(lo, hi):
            rows = slice(r * SUBLANES, (r + 1) * SUBLANES)
            a = abuf[rows, :]
            s = sbuf[rows, :]
            for sh in (1, 2, 4):
                a_prev = jnp.where(sub >= sh, pltpu.roll(a, sh, 0), 1.0)
                s_prev = jnp.where(sub >= sh, pltpu.roll(s, sh, 0), 0.0)
                s = s + a * s_prev
                a = a * a_prev
            hs_r = s + a * h
            hbuf[rows, :] = hs_r
            h = hs_r[SUBLANES - 1:SUBLANES, :]
        return h

    def query_side(qf):
        qt = qf.T
        qt_ref[...] = (qt * (SCALE * LOG2E)).astype(BF16)
        q_hi, q_lo = _split_bf16(qt)
        km_hi, km_lo = _split_bf16(kmt[...])
        nk = kmt.shape[0]
        g2 = jnp.dot(jnp.concatenate([km_hi, km_lo], axis=0), q_hi, preferred_element_type=F32)
        gate = g2[:nk] + g2[nk:] + jnp.dot(km_hi, q_lo, preferred_element_type=F32)
        gn = [gate[n * N_HEADS:(n + 1) * N_HEADS, :] for n in range(n_blk)]
        for n in range(n_blk):
            cnt = jnp.zeros(gn[n].shape, jnp.int32)
            for m in range(n_blk):
                if m == n:
                    continue
                beats = (gn[m] >= gn[n]) if m < n else (gn[m] > gn[n])
                cnt = cnt + jnp.where(beats, 1, 0) * jnp.where(m < i, 1, 0)
            keep = jnp.where(cnt < MOBA_TOPK, 1.0, 0.0) * jnp.where(n < i, 1.0, 0.0)
            selt_ref[n * N_HEADS:(n + 1) * N_HEADS, :] = keep.astype(F32)

    def key_side(kf):
        k_ref[...] = kf
        for hd in range(N_HEADS):
            kb_ref[hd] = kf[:, hd * HEAD_DIM:(hd + 1) * HEAD_DIM].astype(BF16)
        kmean = jnp.sum(kf, axis=0, keepdims=True) * (1.0 / MOBA_BLOCK)
        head_of_lane = lax.broadcasted_iota(jnp.int32, (N_HEADS, D_MODEL), 1) // HEAD_DIM
        row = lax.broadcasted_iota(jnp.int32, (N_HEADS, D_MODEL), 0)
        kmt[pl.ds(pl.multiple_of(i * N_HEADS, N_HEADS), N_HEADS), :] = jnp.where(
            head_of_lane == row, jnp.broadcast_to(kmean, (N_HEADS, D_MODEL)), 0.0)

    u = proj(0)
    qf = proj(2)
    stream.sums(0, 0, 2)
    ubuf[SUBLANES:SUBLANES + tt, :] = u
    uc = cb_ref[...] + cw_ref[CONV_W - 1:CONV_W, :] * u
    for j in range(CONV_W - 1):
        s = SUBLANES - (CONV_W - 1) + j
        uc = uc + cw_ref[j:j + 1, :] * ubuf[s:s + tt, :]
    cs_ref[...] = ubuf[SUBLANES + tt - (CONV_W - 1):SUBLANES + tt, :]
    ubuf[0:SUBLANES, :] = ubuf[tt:tt + SUBLANES, :]
    neg_c_sp = -_softplus(-lam_ref[...])
    ucb = uc.astype(BF16)
    kf = proj(3)
    stream.sums(0, 1, 2)
    stream.end(0)

    stream.begin(1)
    query_side(qf)
    gz = gate_dots(ucb, 0, 2)
    vf = proj(4)
    stream.sums(1, 0, 2)
    key_side(kf)
    rglru_elem(gz, uc, neg_c_sp, 0)
    stream.sums(1, 1, 2)
    stream.end(1)

    stream.begin(2)
    gz = gate_dots(ucb, 2, 4)
    pre_a = proj(5)
    stream.sums(2, 0, 2)
    v_ref[...] = vf
    vt_ref[...] = vf.T.astype(BF16)
    rglru_elem(gz, uc, neg_c_sp, 2)
    gz = gate_dots(ucb, 4, 6)
    pre_b = proj(6)
    stream.sums(2, 1, 2)
    sga_ref[...] = _sigmoid(pre_a).astype(sga_ref.dtype)
    rglru_elem(gz, uc, neg_c_sp, 4)
    stream.end(2)

    stream.begin(3)
    gz = gate_dots(ucb, 6, RNN_BLOCKS)
    g = proj(1)
    stream.sums(3, 0, 2)
    sgb_ref[...] = _sigmoid(pre_b).astype(sgb_ref.dtype)
    rglru_elem(gz, uc, neg_c_sp, 6)
    stream.sums(3, 1, 2)
    h = scan_groups(hcar[0:1, :], 0, tt // SUBLANES)
    hcar[0:1, :] = h
    h_ref[...] = h
    rnn_ref[...] = (hbuf[...] * _gelu_tanh(g)).astype(BF16)
    stream.end(3)
    stream.drain()


def _const_spec(shape):
    nd = len(shape)
    return pl.BlockSpec(shape, lambda *_: (0,) * nd, pipeline_mode=pl.Buffered(1))


def _front_prompt(x, w_in, conv_w, conv_b, w_gate, b_a, b_x, lam, page_table, cache_k):
    b, t, _ = x.shape
    tt = MOBA_BLOCK
    nt = t // tt
    blocks_per_step = STREAM_CHUNKS * (FRONT_RING // 2) // PAGES_PER_BLOCK
    tile = pl.BlockSpec((None, tt, D_MODEL), lambda bi, ti, *_: (bi, ti, 0))
    ttile = pl.BlockSpec((None, D_MODEL, tt), lambda bi, ti, *_: (bi, 0, ti))
    big = lambda dt: jax.ShapeDtypeStruct((b, t, D_MODEL), dt)
    tbig = jax.ShapeDtypeStruct((b, D_MODEL, t), BF16)
    hm_tile = pl.BlockSpec((None, N_HEADS, tt, HEAD_DIM), lambda bi, ti, *_: (bi, 0, ti, 0))
    hm_big = jax.ShapeDtypeStruct((b, N_HEADS, t, HEAD_DIM), BF16)
    out_shape = (big(BF16), tbig, big(F32), big(F32), hm_big, tbig, big(BF16), big(BF16),
                 jax.ShapeDtypeStruct((b, nt, nt * N_HEADS, tt), F32),
                 jax.ShapeDtypeStruct((b, 1, D_MODEL), F32),
                 jax.ShapeDtypeStruct((b, CONV_W - 1, D_MODEL), F32),
                 jax.ShapeDtypeStruct((b * nt * blocks_per_step, N_HEADS, HEAD_DIM), F32))
    out_specs = (tile, ttile, tile, tile, hm_tile, ttile, tile, tile,
                 pl.BlockSpec((None, None, nt * N_HEADS, tt), lambda bi, ti, *_: (bi, ti, 0, 0)),
                 pl.BlockSpec((None, 1, D_MODEL), lambda bi, ti, *_: (bi, 0, 0)),
                 pl.BlockSpec((None, CONV_W - 1, D_MODEL), lambda bi, ti, *_: (bi, 0, 0)),
                 pl.BlockSpec((blocks_per_step, N_HEADS, HEAD_DIM), lambda bi, ti, *_: (bi * nt + ti, 0, 0)))
    in_specs = [tile, pl.BlockSpec(memory_space=pl.ANY),
                _const_spec(w_in.shape), _const_spec(conv_w.shape), _const_spec(conv_b.shape),
                _const_spec(w_gate.shape), _const_spec(b_a.shape), _const_spec(b_x.shape), _const_spec(lam.shape)]
    grid_spec = pltpu.PrefetchScalarGridSpec(
        num_scalar_prefetch=1,
        grid=(b, nt),
        in_specs=in_specs,
        out_specs=out_specs,
        scratch_shapes=[pltpu.VMEM((SUBLANES + tt, D_MODEL), F32),
                        pltpu.VMEM((tt, D_MODEL), F32),
                        pltpu.VMEM((tt, D_MODEL), F32),
                        pltpu.VMEM((tt, D_MODEL), F32),
                        pltpu.VMEM((SUBLANES, D_MODEL), F32),
                        pltpu.VMEM((nt * N_HEADS, D_MODEL), F32),
                        pltpu.VMEM((FRONT_RING, PAGE_SIZE, N_HEADS, HEAD_DIM), F32),
                        pltpu.SemaphoreType.DMA((2,))])
    return pl.pallas_call(
        _front_kernel,
        grid_spec=grid_spec,
        out_shape=out_shape,
        compiler_params=pltpu.CompilerParams(dimension_semantics=("arbitrary", "arbitrary"),
                                             vmem_limit_bytes=VMEM_LIMIT),
        name="front_prompt",
    )(page_table, x, cache_k, w_in, conv_w, conv_b, w_gate, b_a, b_x, lam)


def _attn_block_of_step(qi, n_q):
    return jnp.where(qi % 2 == 0, qi // 2, n_q - 1 - qi // 2)


def _attn_kernel(page_base, pt_ref, qt_ref, kb_ref, vt_ref, selt_ref, ck_ref, o_ref, ksum_ref, ring, sems):
    tq = qt_ref.shape[1]
    n_q = kb_ref.shape[1] // MOBA_BLOCK
    i = _attn_block_of_step(pl.program_id(1), n_q)
    n_groups = N_HEADS // HEADS_PER_ITER
    step = pl.program_id(0) * pl.num_programs(1) + pl.program_id(1)
    stream = _PageStream(pt_ref, ck_ref, ring, sems, ksum_ref, step,
                         pl.num_programs(0) * pl.num_programs(1), page_base, n_groups)
    stream.prime()
    key_idx = lax.broadcasted_iota(jnp.int32, (MOBA_BLOCK, tq), 0)
    qry_idx = lax.broadcasted_iota(jnp.int32, (MOBA_BLOCK, tq), 1)
    causal = key_idx <= qry_idx

    def head_rows(h):
        return pl.ds(pl.multiple_of(h * HEAD_DIM, HEAD_DIM), HEAD_DIM)

    def masked_scores(h, n_past):
        nk = (n_past + 1) * MOBA_BLOCK
        s = jnp.dot(kb_ref[h, 0:nk, :], qt_ref[head_rows(h), :], preferred_element_type=F32)
        parts = []
        for j in range(n_past):
            keep = selt_ref[pl.ds(j * N_HEADS + h, 1), :]
            parts.append(jnp.where(keep > 0.5, s[j * MOBA_BLOCK:(j + 1) * MOBA_BLOCK, :], NEG))
        parts.append(jnp.where(causal, s[n_past * MOBA_BLOCK:, :], NEG))
        return jnp.concatenate(parts, axis=0) if n_past else parts[0]

    def softmax_pv(h, s):
        nk = s.shape[0]
        m = jnp.max(s, axis=0, keepdims=True)
        p = jnp.exp2(s - m)
        l = jnp.sum(p, axis=0, keepdims=True)
        ot = jnp.dot(vt_ref[head_rows(h), 0:nk], p.astype(BF16), preferred_element_type=F32) * (1.0 / l)
        o_ref[h] = ot.T.astype(BF16)

    def attend(n_past):
        def head_group(g, carry):
            heads = [g * HEADS_PER_ITER + u for u in range(HEADS_PER_ITER)]
            stream.begin(g)
            scores = [masked_scores(h, n_past) for h in heads]
            for k, (h, s) in enumerate(zip(heads, scores)):
                stream.sums(g, k, HEADS_PER_ITER)
                softmax_pv(h, s)
            stream.end(g)
            return carry
        lax.fori_loop(0, n_groups, head_group, 0)

    for v in range(n_q):
        @pl.when(i == v)
        def _(v=v):
            attend(v)
    stream.drain()


def _attn_prompt(qt, kb, vt, selt, page_table, cache_k, page_base):
    b, _, t = qt.shape
    tq = MOBA_BLOCK
    nq = t // tq
    blocks_per_step = (N_HEADS // HEADS_PER_ITER) * (ATTN_RING // 2) // PAGES_PER_BLOCK
    blk = functools.partial(_attn_block_of_step, n_q=nq)
    grid_spec = pltpu.PrefetchScalarGridSpec(
        num_scalar_prefetch=1,
        grid=(b, nq),
        in_specs=[pl.BlockSpec((None, D_MODEL, tq), lambda bi, qi, *_: (bi, 0, blk(qi))),
                  pl.BlockSpec((None, N_HEADS, t, HEAD_DIM), lambda bi, qi, *_: (bi, 0, 0, 0)),
                  pl.BlockSpec((None, D_MODEL, t), lambda bi, qi, *_: (bi, 0, 0)),
                  pl.BlockSpec((None, None, nq * N_HEADS, tq), lambda bi, qi, *_: (bi, blk(qi), 0, 0)),
                  pl.BlockSpec(memory_space=pl.ANY)],
        out_specs=(pl.BlockSpec((None, N_HEADS, tq, HEAD_DIM), lambda bi, qi, *_: (bi, 0, blk(qi), 0)),
                   pl.BlockSpec((blocks_per_step, N_HEADS, HEAD_DIM), lambda bi, qi, *_: (bi * nq + qi, 0, 0))),
        scratch_shapes=[pltpu.VMEM((ATTN_RING, PAGE_SIZE, N_HEADS, HEAD_DIM), F32),
                        pltpu.SemaphoreType.DMA((2,))])
    return pl.pallas_call(
        functools.partial(_attn_kernel, page_base),
        grid_spec=grid_spec,
        out_shape=(jax.ShapeDtypeStruct((b, N_HEADS, t, HEAD_DIM), BF16),
                   jax.ShapeDtypeStruct((b * nq * blocks_per_step, N_HEADS, HEAD_DIM), F32)),
        compiler_params=pltpu.CompilerParams(dimension_semantics=("arbitrary", "arbitrary"),
                                             vmem_limit_bytes=VMEM_LIMIT),
        name="attn_prompt",
    )(page_table, qt, kb, vt, selt, cache_k)


def _post_body(rnn_ref, o_ref, sga_ref, sgb_ref, x_ref, wbr_ref, wba_ref, wout_ref, wup_ref, wdn_ref,
               g1_ref, b1_ref, bup_ref, bdn_ref, g2_ref, b2_ref, y_ref, stream):
    stream.begin(0)
    y_rnn = jnp.dot(rnn_ref[...], wbr_ref[...], preferred_element_type=F32)
    stream.sums(0, 0, 2)
    o = jnp.concatenate([o_ref[h] for h in range(N_HEADS)], axis=-1)
    y_att = jnp.dot(o, wba_ref[...], preferred_element_type=F32)
    stream.sums(0, 1, 2)
    mix = sga_ref[...] * y_rnn + sgb_ref[...] * y_att
    stream.end(0)
    stream.begin(1)
    t1 = ALPHA * x_ref[...] + jnp.dot(mix.astype(BF16), wout_ref[...], preferred_element_type=F32)
    stream.sums(1, 0, 2)
    x1 = _layer_norm(t1, g1_ref[...], b1_ref[...])
    x1b = x1.astype(BF16)
    acc = ALPHA * x1 + bdn_ref[...]
    stream.sums(1, 1, 2)
    stream.end(1)
    n_chunks = D_FF // D_MODEL
    for c in range(n_chunks):
        stage, part = 2 + c // 2, c % 2
        if part == 0:
            stream.begin(stage)
        cs = slice(c * D_MODEL, (c + 1) * D_MODEL)
        hid = jnp.maximum(jnp.dot(x1b, wup_ref[:, cs], preferred_element_type=F32) + bup_ref[:, cs], 0.0)
        stream.sums(stage, 2 * part, 4)
        acc = acc + jnp.dot((hid * hid).astype(BF16), wdn_ref[cs, :], preferred_element_type=F32)
        stream.sums(stage, 2 * part + 1, 4)
        if part == 1 and stage == 2:
            stream.end(stage)
    y_ref[...] = _layer_norm(acc, g2_ref[...], b2_ref[...])
    stream.end(3)


def _post_kernel(*refs):
    _post_body(*refs, stream=_NoStream())


def _post_stream_kernel(page_base, pt_ref, rnn_ref, o_ref, sga_ref, sgb_ref, x_ref, ck_ref, *rest):
    consts, (y_ref, ksum_ref, ring, sems) = rest[:11], rest[11:]
    stream = _PageStream(pt_ref, ck_ref, ring, sems, ksum_ref, pl.program_id(0), pl.num_programs(0),
                         page_base, STREAM_CHUNKS)
    stream.prime()
    _post_body(rnn_ref, o_ref, sga_ref, sgb_ref, x_ref, *consts, y_ref, stream=stream)
    stream.drain()


def _post(rnn, o, sga, sgb, x, wbr, wba, wout, wup, wdn, g1, b1, bup, bdn, g2, b2, tm, stream=None):
    m = x.shape[0]
    n_steps = m // tm
    tiles_per_seq = o.shape[2] // tm
    assert o.shape[0] * o.shape[2] == m
    tile = pl.BlockSpec((tm, D_MODEL), lambda r, *_: (r, 0))
    o_tile = pl.BlockSpec((None, N_HEADS, tm, HEAD_DIM),
                          lambda r, *_: (r // tiles_per_seq, 0, r % tiles_per_seq, 0))
    consts = (wbr, wba, wout, wup, wdn, g1, b1, bup, bdn, g2, b2)
    params = pltpu.CompilerParams(dimension_semantics=("arbitrary",), vmem_limit_bytes=VMEM_LIMIT)
    y_shape = jax.ShapeDtypeStruct((m, D_MODEL), F32)
    if stream is None:
        return pl.pallas_call(
            _post_kernel,
            grid=(n_steps,),
            in_specs=[tile, o_tile, tile, tile, tile] + [_const_spec(c.shape) for c in consts],
            out_specs=tile,
            out_shape=y_shape,
            compiler_params=params,
            name="post_m%d" % m,
        )(rnn, o, sga, sgb, x, *consts)
    page_table, cache_k, page_base = stream
    blocks_per_step = STREAM_CHUNKS * (POST_RING // 2) // PAGES_PER_BLOCK
    n_blocks = blocks_per_step * n_steps
    assert page_base + n_blocks * PAGES_PER_BLOCK == page_table.size, "front + post shares must cover all pages"
    grid_spec = pltpu.PrefetchScalarGridSpec(
        num_scalar_prefetch=1,
        grid=(n_steps,),
        in_specs=([tile, o_tile, tile, tile, tile, pl.BlockSpec(memory_space=pl.ANY)]
                  + [_const_spec(c.shape) for c in consts]),
        out_specs=(tile, pl.BlockSpec((blocks_per_step, N_HEADS, HEAD_DIM), lambda r, *_: (r, 0, 0))),
        scratch_shapes=[pltpu.VMEM((POST_RING, PAGE_SIZE, N_HEADS, HEAD_DIM), F32),
                        pltpu.SemaphoreType.DMA((2,))])
    return pl.pallas_call(
        functools.partial(_post_stream_kernel, page_base),
        grid_spec=grid_spec,
        out_shape=(y_shape, jax.ShapeDtypeStruct((n_blocks, N_HEADS, HEAD_DIM), F32)),
        compiler_params=params,
        name="post_stream_m%d" % m,
    )(page_table, rnn, o, sga, sgb, x, cache_k, *consts)


def _front_sample_kernel(x_ref, p0_ref, p1_ref, p2_ref, h0_ref, w_ref, cw_ref, cb_ref, wg_ref, ba_ref,
                         bx_ref, lam_ref,
                         rnn_ref, q_ref, k_ref, v_ref, sga_ref, sgb_ref, h_ref, u_ref):
    xb = x_ref[...].astype(BF16)

    def proj(j):
        return jnp.dot(xb, w_ref[:, j * D_MODEL:(j + 1) * D_MODEL], preferred_element_type=F32)

    u = proj(0)
    u_ref[...] = u
    uc = (cb_ref[...] + cw_ref[0:1, :] * p0_ref[...] + cw_ref[1:2, :] * p1_ref[...]
          + cw_ref[2:3, :] * p2_ref[...] + cw_ref[3:4, :] * u)
    neg_c_sp = -_softplus(-lam_ref[...])
    ucb = uc.astype(BF16)
    g = proj(1)
    for n in range(RNN_BLOCKS):
        sl = slice(n * RNN_BW, (n + 1) * RNN_BW)
        a, uin = _rglru_block(uc[:, sl], ucb[:, sl], wg_ref[n], ba_ref[:, sl], bx_ref[:, sl],
                              neg_c_sp[:, sl])
        h = a * h0_ref[:, sl] + uin
        h_ref[:, sl] = h
        rnn_ref[:, sl] = (h * _gelu_tanh(g[:, sl])).astype(BF16)
    q_ref[...] = proj(2)
    k_ref[...] = proj(3)
    v_ref[...] = proj(4)
    sga_ref[...] = _sigmoid(proj(5))
    sgb_ref[...] = _sigmoid(proj(6))


def _front_sample(x, p0, p1, p2, h0, w_in, conv_w, conv_b, w_gate, b_a, b_x, lam):
    m = x.shape[0]
    args = (x, p0, p1, p2, h0, w_in, conv_w, conv_b, w_gate, b_a, b_x, lam)
    row = lambda dt: jax.ShapeDtypeStruct((m, D_MODEL), dt)
    return pl.pallas_call(
        _front_sample_kernel,
        grid=(1,),
        in_specs=[_const_spec(a.shape) for a in args],
        out_specs=tuple(pl.BlockSpec((m, D_MODEL), lambda r: (0, 0)) for _ in range(8)),
        out_shape=(row(BF16), row(F32), row(F32), row(F32), row(F32), row(F32), row(F32), row(F32)),
        compiler_params=pltpu.CompilerParams(dimension_semantics=("arbitrary",),
                                             vmem_limit_bytes=VMEM_LIMIT),
        name="front_sample",
    )(*args)


def _sample_attn_kernel(pt_ref, ksum_ref, q_ref, kn_ref, vn_ref, ck_ref, cv_ref, o_ref,
                        kbuf, vbuf, ksem, vsem):
    b = pl.program_id(0)
    n_seq = pl.num_programs(0)
    n_blocks = ksum_ref.shape[0]
    n_gather = MOBA_TOPK * PAGES_PER_BLOCK
    par = lax.rem(b, 2)

    gate = jnp.sum(ksum_ref[...] * (1.0 / MOBA_BLOCK) * q_ref[b][None], axis=-1, keepdims=True)
    blk = lax.broadcasted_iota(jnp.int32, gate.shape, 0)

    def gather_copy(src_ref, dst_ref, sem, page, h, slot, buf):
        return pltpu.make_async_copy(src_ref.at[0, page, :, h, :], dst_ref.at[buf, h, slot], sem.at[buf])

    for t in range(MOBA_TOPK):
        mx = jnp.max(gate, axis=0, keepdims=True)
        idx = jnp.min(jnp.where(gate == mx, blk, n_blocks), axis=0, keepdims=True)
        gate = jnp.where(blk == idx, -jnp.inf, gate)
        for h in range(N_HEADS):
            sel = idx[0, h, 0]
            for pg in range(PAGES_PER_BLOCK):
                page = pt_ref[b, sel * PAGES_PER_BLOCK + pg]
                slot = t * PAGES_PER_BLOCK + pg
                gather_copy(ck_ref, kbuf, ksem, page, h, slot, par).start()
                gather_copy(cv_ref, vbuf, vsem, page, h, slot, par).start()

    def attend(seq, buf):
        for h in range(N_HEADS):
            for slot in range(n_gather):
                gather_copy(ck_ref, kbuf, ksem, 0, h, slot, buf).wait()
                gather_copy(cv_ref, vbuf, vsem, 0, h, slot, buf).wait()
        q = q_ref[seq]
        kn = kn_ref[seq]
        vn = vn_ref[seq]
        for h in range(N_HEADS):
            qh = q[h:h + 1, :]
            kh = kbuf[buf, h].reshape(n_gather * PAGE_SIZE, HEAD_DIM)
            vh = vbuf[buf, h].reshape(n_gather * PAGE_SIZE, HEAD_DIM)
            s = jnp.sum(kh * qh, axis=-1, keepdims=True) * SCALE
            s_new = jnp.sum(kn[h:h + 1, :] * qh, axis=-1, keepdims=True) * SCALE
            m = jnp.maximum(jnp.max(s, axis=0, keepdims=True), s_new)
            p = jnp.exp(s - m)
            p_new = jnp.exp(s_new - m)
            l = jnp.sum(p, axis=0, keepdims=True) + p_new
            o = jnp.sum(p * vh, axis=0, keepdims=True) + p_new * vn[h:h + 1, :]
            o_ref[seq, h:h + 1, :] = o / l

    @pl.when(b > 0)
    def _():
        attend(b - 1, 1 - par)

    @pl.when(b == n_seq - 1)
    def _():
        attend(b, par)


def _sample_attn(page_table, ksum, q, k_new, v_new, cache_k, cache_v):
    n_seq, n_pages = page_table.shape
    assert (n_pages * PAGE_SIZE) % MOBA_BLOCK == 0, "own-block cached prefix is not supported"
    n_blocks = n_pages // PAGES_PER_BLOCK
    assert n_blocks >= MOBA_TOPK and ksum.shape[0] == n_seq * n_blocks
    n_gather = MOBA_TOPK * PAGES_PER_BLOCK
    whole = pl.BlockSpec((n_seq, N_HEADS, HEAD_DIM), lambda s, pt: (0, 0, 0))
    anyspec = pl.BlockSpec(memory_space=pl.ANY)
    gathered = pltpu.VMEM((2, N_HEADS, n_gather, PAGE_SIZE, HEAD_DIM), F32)
    grid_spec = pltpu.PrefetchScalarGridSpec(
        num_scalar_prefetch=1,
        grid=(n_seq,),
        in_specs=[pl.BlockSpec((n_blocks, N_HEADS, HEAD_DIM), lambda s, pt: (s, 0, 0)),
                  whole, whole, whole, anyspec, anyspec],
        out_specs=whole,
        scratch_shapes=[gathered, gathered,
                        pltpu.SemaphoreType.DMA((2,)),
                        pltpu.SemaphoreType.DMA((2,))])
    return pl.pallas_call(
        _sample_attn_kernel,
        grid_spec=grid_spec,
        out_shape=jax.ShapeDtypeStruct((n_seq, N_HEADS, HEAD_DIM), F32),
        compiler_params=pltpu.CompilerParams(dimension_semantics=("arbitrary",),
                                             vmem_limit_bytes=VMEM_LIMIT),
        name="attn_sample",
    )(page_table, ksum, q, k_new, v_new, cache_k, cache_v)


def kernel(x_prompt, x_sample, cache_k, cache_v, state_h, state_conv, page_table, w_in, conv_w, conv_b,
           w_rg_a, b_rg_a, w_rg_x, b_rg_x, lru_lambda, w_br_rnn, w_br_attn, w_out, ln1_g, ln1_b,
           w_up, b_up, w_down, b_down, ln2_g, ln2_b):
    assert w_in.shape[0] == 1, "single-layer trunk"
    b, t, _ = x_prompt.shape
    db = x_sample.shape[0]
    assert x_sample.shape[1] == 1 and t % MOBA_BLOCK == 0

    w_in_b = w_in[0].astype(BF16)
    w_gate = jnp.concatenate([w_rg_a[0], w_rg_x[0]], axis=-1).astype(BF16)
    cw, cb = conv_w[0], conv_b
    post_w = (w_br_rnn[0].astype(BF16), w_br_attn[0].astype(BF16), w_out[0].astype(BF16),
              w_up[0].astype(BF16), w_down[0].astype(BF16), ln1_g, ln1_b, b_up, b_down, ln2_g, ln2_b)

    (rnn_p, qt_p, k_p, v_p, kb_p, vt_p, sga_p, sgb_p, selt_p, h_p, cs_p, ksum_a) = _front_prompt(
        x_prompt, w_in_b, cw, cb, w_gate, b_rg_a, b_rg_x, lru_lambda, page_table, cache_k)
    pages_a = ksum_a.shape[0] * PAGES_PER_BLOCK
    o_p, ksum_b = _attn_prompt(qt_p, kb_p, vt_p, selt_p, page_table, cache_k, pages_a)
    pages_ab = pages_a + ksum_b.shape[0] * PAGES_PER_BLOCK
    flat = lambda a: a.reshape(b * t, D_MODEL)
    y_p, ksum_c = _post(flat(rnn_p), o_p, flat(sga_p), flat(sgb_p), flat(x_prompt), *post_w, tm=256,
                        stream=(page_table, cache_k, pages_ab))
    ksum = jnp.concatenate([ksum_a, ksum_b, ksum_c], axis=0)

    xs = x_sample.reshape(db, D_MODEL)
    sc = state_conv[0]
    (rnn_s, q_s, k_s, v_s, sga_s, sgb_s, h_s, u_s) = _front_sample(
        xs, sc[:, 0], sc[:, 1], sc[:, 2], state_h[0], w_in_b, cw, cb, w_gate, b_rg_a, b_rg_x, lru_lambda)
    heads = lambda a: a.reshape(db, N_HEADS, HEAD_DIM)
    o_s = _sample_attn(page_table, ksum, heads(q_s), heads(k_s), heads(v_s), cache_k, cache_v)
    o_s_hm = jnp.transpose(o_s, (1, 0, 2)).astype(BF16)[None]
    y_s = _post(rnn_s, o_s_hm, sga_s, sgb_s, xs, *post_w, tm=db)
    cs_s = jnp.concatenate([sc[:, 1:], u_s[:, None, :]], axis=1)

    kv_p = lambda a: a.reshape(1, b, t, N_HEADS, HEAD_DIM)
    kv_s = lambda a: a.reshape(1, db, 1, N_HEADS, HEAD_DIM)
    return (y_p.reshape(b, t, D_MODEL), y_s.reshape(db, 1, D_MODEL), kv_p(k_p), kv_p(v_p),
            h_p.reshape(1, b, D_MODEL), cs_p[None],
            kv_s(k_s), kv_s(v_s), h_s[None], cs_s[None])
```

```python
import functools
import math

import jax
import jax.numpy as jnp
from jax import lax
from jax.experimental import pallas as pl
from jax.experimental.pallas import tpu as pltpu

F32 = jnp.float32
BF16 = jnp.bfloat16

D_MODEL = 1024
N_HEADS = 8
HEAD_DIM = 128
RNN_BLOCKS = 8
RNN_BW = 128
CONV_W = 4
LRU_C = 8.0
MOBA_BLOCK = 256
MOBA_TOPK = 3
PAGE_SIZE = 128
PAGES_PER_BLOCK = MOBA_BLOCK // PAGE_SIZE
D_FF = 4096
ALPHA = 2.0 ** 0.25
LN_EPS = 1e-5
NEG = -1e30
SCALE = HEAD_DIM ** -0.5
LOG2E = math.log2(math.e)

SUBLANES = 8
VMEM_LIMIT = 56 * 1024 * 1024
HEADS_PER_ITER = 4


def _sigmoid(x):
    return 0.5 * jnp.tanh(0.5 * x) + 0.5


def _softplus(x):
    return jnp.maximum(x, 0.0) + jnp.log1p(jnp.exp(-jnp.abs(x)))


def _gelu_tanh(x):
    c = math.sqrt(2.0 / math.pi)
    return x * (0.5 * (1.0 + jnp.tanh(c * (x + 0.044715 * (x * x * x)))))


def _layer_norm(x, g, b):
    mu = jnp.mean(x, axis=-1, keepdims=True)
    xc = x - mu
    var = jnp.mean(xc * xc, axis=-1, keepdims=True)
    return xc * lax.rsqrt(var + LN_EPS) * g + b


def _split_bf16(x):
    hi = x.astype(BF16)
    lo = (x - hi.astype(F32)).astype(BF16)
    return hi, lo


def _rglru_gate_dot(ucb_blk, wg):
    return jnp.dot(ucb_blk, wg, preferred_element_type=F32)


def _rglru_block(uc_blk, ucb_blk, wg, ba, bx, neg_c_sp):
    return _rglru_elem(_rglru_gate_dot(ucb_blk, wg), uc_blk, ba, bx, neg_c_sp)


def _rglru_elem(gz, uc_blk, ba, bx, neg_c_sp):
    r = _sigmoid(gz[:, :RNN_BW] + ba)
    ig = _sigmoid(gz[:, RNN_BW:] + bx)
    log_a = (LRU_C * r) * neg_c_sp
    a = jnp.exp(log_a)
    t = jnp.tanh(log_a)
    mult = jnp.sqrt(-2.0 * t / (1.0 - t))
    return a, mult * (ig * uc_blk)


def _page_sum(page):
    rows = page.shape[0]
    parts = jnp.sum(page.reshape(SUBLANES, rows // SUBLANES, N_HEADS, HEAD_DIM), axis=1)
    return jnp.sum(parts, axis=0)


class _PageStream:
    def __init__(self, pt_ref, ck_ref, ring, sems, ksum_ref, step, n_steps, page_base, n_chunks):
        self.pt_ref, self.ck_ref, self.ring, self.sems, self.ksum_ref = pt_ref, ck_ref, ring, sems, ksum_ref
        self.step, self.n_steps, self.page_base, self.n_chunks = step, n_steps, page_base, n_chunks
        self.n_ring = ring.shape[0]
        self.half = self.n_ring // 2
        self.per_step = ksum_ref.shape[0] * PAGES_PER_BLOCK
        assert self.per_step == n_chunks * self.half and n_chunks % 2 == 0
        assert self.half % PAGES_PER_BLOCK == 0
        self.span = self.per_step * n_steps

    def _copy(self, rel, half_idx, s):
        n_pages = self.pt_ref.shape[1]
        g = self.page_base + lax.rem(jnp.asarray(rel, jnp.int32), jnp.asarray(self.span, jnp.int32))
        page = self.pt_ref[g // n_pages, lax.rem(g, n_pages)]
        return pltpu.make_async_copy(self.ck_ref.at[0, page], self.ring.at[half_idx * self.half + s],
                                     self.sems.at[half_idx])

    @staticmethod
    def _half_of(c):
        return c % 2 if isinstance(c, int) else lax.rem(c, 2)

    def prime(self):
        @pl.when(self.step == 0)
        def _():
            for s in range(self.n_ring):
                self._copy(s, s // self.half, s % self.half).start()

    def begin(self, c):
        for s in range(self.half):
            self._copy(0, self._half_of(c), s).wait()

    def sums(self, c, part, n_parts):
        blocks = self.half // PAGES_PER_BLOCK
        base = self._half_of(c) * self.half
        for blk in range(part * blocks // n_parts, (part + 1) * blocks // n_parts):
            acc = _page_sum(self.ring[base + blk * PAGES_PER_BLOCK])
            for t in range(1, PAGES_PER_BLOCK):
                acc = acc + _page_sum(self.ring[base + blk * PAGES_PER_BLOCK + t])
            self.ksum_ref[c * blocks + blk] = acc

    def end(self, c):
        for s in range(self.half):
            rel = self.step * self.per_step + c * self.half + s + self.n_ring
            self._copy(rel, self._half_of(c), s).start()

    def drain(self):
        @pl.when(self.step == self.n_steps - 1)
        def _():
            for s in range(self.n_ring):
                self._copy(0, s // self.half, s % self.half).wait()


class _ChunkOffset:
    def __init__(self, stream, base):
        self.stream, self.base = stream, base

    def begin(self, c): self.stream.begin(self.base + c)
    def sums(self, c, part, n_parts): self.stream.sums(self.base + c, part, n_parts)
    def end(self, c): self.stream.end(self.base + c)


class _NoStream:
    def prime(self): pass
    def begin(self, c): pass
    def sums(self, c, part, n_parts): pass
    def end(self, c): pass
    def drain(self): pass


STREAM_CHUNKS = 4
POST_SUB_ROWS = 256
FRONT_RING = 8
POST_RING = 12
ATTN_RING = 24


def _front_kernel(pt_ref, x_ref, ck_ref, w_ref, cw_ref, cb_ref, wg_ref, ba_ref, bx_ref, lam_ref,
                  rnn_ref, qt_ref, k_ref, v_ref, kb_ref, vt_ref, sga_ref, sgb_ref, selt_ref,
                  h_ref, cs_ref, ksum_ref,
                  ubuf, abuf, sbuf, hbuf, hcar, kmt, ring, sems):
    i = pl.program_id(1)
    tt = x_ref.shape[0]
    n_blk = kmt.shape[0] // N_HEADS
    step = pl.program_id(0) * pl.num_programs(1) + i
    stream = _PageStream(pt_ref, ck_ref, ring, sems, ksum_ref, step,
                         pl.num_programs(0) * pl.num_programs(1), 0, STREAM_CHUNKS)
    stream.prime()
    stream.begin(0)

    @pl.when(i == 0)
    def _():
        ubuf[0:SUBLANES, :] = jnp.zeros((SUBLANES, D_MODEL), F32)
        hcar[...] = jnp.zeros(hcar.shape, F32)
        kmt[...] = jnp.zeros(kmt.shape, F32)

    xb = x_ref[...].astype(BF16)

    def proj(j):
        return jnp.dot(xb, w_ref[:, j * D_MODEL:(j + 1) * D_MODEL], preferred_element_type=F32)


    def gate_dots(ucb, lo, hi):
        return [_rglru_gate_dot(ucb[:, n * RNN_BW:(n + 1) * RNN_BW], wg_ref[n]) for n in range(lo, hi)]

    def rglru_elem(gzs, uc, neg_c_sp, lo):
        for k, gz in enumerate(gzs):
            sl = slice((lo + k) * RNN_BW, (lo + k + 1) * RNN_BW)
            a, uin = _rglru_elem(gz, uc[:, sl], ba_ref[:, sl], bx_ref[:, sl], neg_c_sp[:, sl])
            abuf[:, sl] = a
            sbuf[:, sl] = uin

    def scan_groups(h, lo, hi):
        sub = lax.broadcasted_iota(jnp.int32, (SUBLANES, D_MODEL), 0)
        for r in range(lo, hi):
            rows = slice(r * SUBLANES, (r + 1) * SUBLANES)
            a = abuf[rows, :]
            s = sbuf[rows, :]
            for sh in (1, 2, 4):
                a_prev = jnp.where(sub >= sh, pltpu.roll(a, sh, 0), 1.0)
                s_prev = jnp.where(sub >= sh, pltpu.roll(s, sh, 0), 0.0)
                s = s + a * s_prev
                a = a * a_prev
            hs_r = s + a * h
            hbuf[rows, :] = hs_r
            h = hs_r[SUBLANES - 1:SUBLANES, :]
        return h

    def query_side(qf):
        qt = qf.T
        qt_ref[...] = (qt * (SCALE * LOG2E)).astype(BF16)
        q_hi, q_lo = _split_bf16(qt)
        km_hi, km_lo = _split_bf16(kmt[...])
        nk = kmt.shape[0]
        g2 = jnp.dot(jnp.concatenate([km_hi, km_lo], axis=0), q_hi, preferred_element_type=F32)
        gate = g2[:nk] + g2[nk:] + jnp.dot(km_hi, q_lo, preferred_element_type=F32)
        gn = [gate[n * N_HEADS:(n + 1) * N_HEADS, :] for n in range(n_blk)]
        for n in range(n_blk):
            cnt = jnp.zeros(gn[n].shape, jnp.int32)
            for m in range(n_blk):
                if m == n:
                    continue
                beats = (gn[m] >= gn[n]) if m < n else (gn[m] > gn[n])
                cnt = cnt + jnp.where(beats, 1, 0) * jnp.where(m < i, 1, 0)
            keep = jnp.where(cnt < MOBA_TOPK, 1.0, 0.0) * jnp.where(n < i, 1.0, 0.0)
            selt_ref[n * N_HEADS:(n + 1) * N_HEADS, :] = keep.astype(F32)

    def key_side(kf):
        k_ref[...] = kf
        for hd in range(N_HEADS):
            kb_ref[hd] = kf[:, hd * HEAD_DIM:(hd + 1) * HEAD_DIM].astype(BF16)
        kmean = jnp.sum(kf, axis=0, keepdims=True) * (1.0 / MOBA_BLOCK)
        head_of_lane = lax.broadcasted_iota(jnp.int32, (N_HEADS, D_MODEL), 1) // HEAD_DIM
        row = lax.broadcasted_iota(jnp.int32, (N_HEADS, D_MODEL), 0)
        kmt[pl.ds(pl.multiple_of(i * N_HEADS, N_HEADS), N_HEADS), :] = jnp.where(
            head_of_lane == row, jnp.broadcast_to(kmean, (N_HEADS, D_MODEL)), 0.0)

    u = proj(0)
    qf = proj(2)
    stream.sums(0, 0, 2)
    ubuf[SUBLANES:SUBLANES + tt, :] = u
    uc = cb_ref[...] + cw_ref[CONV_W - 1:CONV_W, :] * u
    for j in range(CONV_W - 1):
        s = SUBLANES - (CONV_W - 1) + j
        uc = uc + cw_ref[j:j + 1, :] * ubuf[s:s + tt, :]
    cs_ref[...] = ubuf[SUBLANES + tt - (CONV_W - 1):SUBLANES + tt, :]
    ubuf[0:SUBLANES, :] = ubuf[tt:tt + SUBLANES, :]
    neg_c_sp = -_softplus(-lam_ref[...])
    ucb = uc.astype(BF16)
    kf = proj(3)
    stream.sums(0, 1, 2)
    stream.end(0)

    stream.begin(1)
    query_side(qf)
    gz = gate_dots(ucb, 0, 2)
    vf = proj(4)
    stream.sums(1, 0, 2)
    key_side(kf)
    rglru_elem(gz, uc, neg_c_sp, 0)
    stream.sums(1, 1, 2)
    stream.end(1)

    stream.begin(2)
    gz = gate_dots(ucb, 2, 4)
    pre_a = proj(5)
    stream.sums(2, 0, 2)
    v_ref[...] = vf
    vt_ref[...] = vf.T.astype(BF16)
    rglru_elem(gz, uc, neg_c_sp, 2)
    gz = gate_dots(ucb, 4, 6)
    pre_b = proj(6)
    stream.sums(2, 1, 2)
    sga_ref[...] = _sigmoid(pre_a).astype(sga_ref.dtype)
    rglru_elem(gz, uc, neg_c_sp, 4)
    stream.end(2)

    stream.begin(3)
    gz = gate_dots(ucb, 6, RNN_BLOCKS)
    g = proj(1)
    stream.sums(3, 0, 2)
    sgb_ref[...] = _sigmoid(pre_b).astype(sgb_ref.dtype)
    rglru_elem(gz, uc, neg_c_sp, 6)
    stream.sums(3, 1, 2)
    h = scan_groups(hcar[0:1, :], 0, tt // SUBLANES)
    hcar[0:1, :] = h
    h_ref[...] = h
    rnn_ref[...] = (hbuf[...] * _gelu_tanh(g)).astype(BF16)
    stream.end(3)
    stream.drain()


def _const_spec(shape):
    nd = len(shape)
    return pl.BlockSpec(shape, lambda *_: (0,) * nd, pipeline_mode=pl.Buffered(1))


def _front_prompt(x, w_in, conv_w, conv_b, w_gate, b_a, b_x, lam, page_table, cache_k):
    b, t, _ = x.shape
    tt = MOBA_BLOCK
    nt = t // tt
    blocks_per_step = STREAM_CHUNKS * (FRONT_RING // 2) // PAGES_PER_BLOCK
    tile = pl.BlockSpec((None, tt, D_MODEL), lambda bi, ti, *_: (bi, ti, 0))
    ttile = pl.BlockSpec((None, D_MODEL, tt), lambda bi, ti, *_: (bi, 0, ti))
    big = lambda dt: jax.ShapeDtypeStruct((b, t, D_MODEL), dt)
    tbig = jax.ShapeDtypeStruct((b, D_MODEL, t), BF16)
    hm_tile = pl.BlockSpec((None, N_HEADS, tt, HEAD_DIM), lambda bi, ti, *_: (bi, 0, ti, 0))
    hm_big = jax.ShapeDtypeStruct((b, N_HEADS, t, HEAD_DIM), BF16)
    out_shape = (big(BF16), tbig, big(F32), big(F32), hm_big, tbig, big(BF16), big(BF16),
                 jax.ShapeDtypeStruct((b, nt, nt * N_HEADS, tt), F32),
                 jax.ShapeDtypeStruct((b, 1, D_MODEL), F32),
                 jax.ShapeDtypeStruct((b, CONV_W - 1, D_MODEL), F32),
                 jax.ShapeDtypeStruct((b * nt * blocks_per_step, N_HEADS, HEAD_DIM), F32))
    out_specs = (tile, ttile, tile, tile, hm_tile, ttile, tile, tile,
                 pl.BlockSpec((None, None, nt * N_HEADS, tt), lambda bi, ti, *_: (bi, ti, 0, 0)),
                 pl.BlockSpec((None, 1, D_MODEL), lambda bi, ti, *_: (bi, 0, 0)),
                 pl.BlockSpec((None, CONV_W - 1, D_MODEL), lambda bi, ti, *_: (bi, 0, 0)),
                 pl.BlockSpec((blocks_per_step, N_HEADS, HEAD_DIM), lambda bi, ti, *_: (bi * nt + ti, 0, 0)))
    in_specs = [tile, pl.BlockSpec(memory_space=pl.ANY),
                _const_spec(w_in.shape), _const_spec(conv_w.shape), _const_spec(conv_b.shape),
                _const_spec(w_gate.shape), _const_spec(b_a.shape), _const_spec(b_x.shape), _const_spec(lam.shape)]
    grid_spec = pltpu.PrefetchScalarGridSpec(
        num_scalar_prefetch=1,
        grid=(b, nt),
        in_specs=in_specs,
        out_specs=out_specs,
        scratch_shapes=[pltpu.VMEM((SUBLANES + tt, D_MODEL), F32),
                        pltpu.VMEM((tt, D_MODEL), F32),
                        pltpu.VMEM((tt, D_MODEL), F32),
                        pltpu.VMEM((tt, D_MODEL), F32),
                        pltpu.VMEM((SUBLANES, D_MODEL), F32),
                        pltpu.VMEM((nt * N_HEADS, D_MODEL), F32),
                        pltpu.VMEM((FRONT_RING, PAGE_SIZE, N_HEADS, HEAD_DIM), F32),
                        pltpu.SemaphoreType.DMA((2,))])
    return pl.pallas_call(
        _front_kernel,
        grid_spec=grid_spec,
        out_shape=out_shape,
        compiler_params=pltpu.CompilerParams(dimension_semantics=("arbitrary", "arbitrary"),
                                             vmem_limit_bytes=VMEM_LIMIT),
        name="front_prompt",
    )(page_table, x, cache_k, w_in, conv_w, conv_b, w_gate, b_a, b_x, lam)


def _attn_kernel(page_base, pt_ref, qt_ref, kb_ref, vt_ref, selt_ref, ck_ref, o_ref, ksum_ref, ring, sems):
    i = pl.program_id(1)
    tq = qt_ref.shape[1]
    n_q = kb_ref.shape[1] // MOBA_BLOCK
    n_groups = N_HEADS // HEADS_PER_ITER
    stream = _PageStream(pt_ref, ck_ref, ring, sems, ksum_ref, pl.program_id(0) * pl.num_programs(1) + i,
                         pl.num_programs(0) * pl.num_programs(1), page_base, n_groups)
    stream.prime()
    key_idx = lax.broadcasted_iota(jnp.int32, (MOBA_BLOCK, tq), 0)
    qry_idx = lax.broadcasted_iota(jnp.int32, (MOBA_BLOCK, tq), 1)
    causal = key_idx <= qry_idx

    def head_rows(h):
        return pl.ds(pl.multiple_of(h * HEAD_DIM, HEAD_DIM), HEAD_DIM)

    def masked_scores(h, n_past):
        nk = (n_past + 1) * MOBA_BLOCK
        s = jnp.dot(kb_ref[h, 0:nk, :], qt_ref[head_rows(h), :], preferred_element_type=F32)
        parts = []
        for j in range(n_past):
            keep = selt_ref[pl.ds(j * N_HEADS + h, 1), :]
            parts.append(jnp.where(keep > 0.5, s[j * MOBA_BLOCK:(j + 1) * MOBA_BLOCK, :], NEG))
        parts.append(jnp.where(causal, s[n_past * MOBA_BLOCK:, :], NEG))
        return jnp.concatenate(parts, axis=0) if n_past else parts[0]

    def softmax_pv(h, s):
        nk = s.shape[0]
        m = jnp.max(s, axis=0, keepdims=True)
        p = jnp.exp2(s - m)
        l = jnp.sum(p, axis=0, keepdims=True)
        ot = jnp.dot(vt_ref[head_rows(h), 0:nk], p.astype(BF16), preferred_element_type=F32) * (1.0 / l)
        o_ref[h] = ot.T.astype(BF16)

    def attend(n_past):
        def head_group(g, carry):
            heads = [g * HEADS_PER_ITER + u for u in range(HEADS_PER_ITER)]
            stream.begin(g)
            scores = [masked_scores(h, n_past) for h in heads]
            for k, (h, s) in enumerate(zip(heads, scores)):
                stream.sums(g, k, HEADS_PER_ITER)
                softmax_pv(h, s)
            stream.end(g)
            return carry
        lax.fori_loop(0, n_groups, head_group, 0)

    for v in range(n_q):
        @pl.when(i == v)
        def _(v=v):
            attend(v)
    stream.drain()


def _attn_prompt(qt, kb, vt, selt, page_table, cache_k, page_base):
    b, _, t = qt.shape
    tq = MOBA_BLOCK
    nq = t // tq
    blocks_per_step = (N_HEADS // HEADS_PER_ITER) * (ATTN_RING // 2) // PAGES_PER_BLOCK
    grid_spec = pltpu.PrefetchScalarGridSpec(
        num_scalar_prefetch=1,
        grid=(b, nq),
        in_specs=[pl.BlockSpec((None, D_MODEL, tq), lambda bi, qi, *_: (bi, 0, qi)),
                  pl.BlockSpec((None, N_HEADS, t, HEAD_DIM), lambda bi, qi, *_: (bi, 0, 0, 0)),
                  pl.BlockSpec((None, D_MODEL, t), lambda bi, qi, *_: (bi, 0, 0)),
                  pl.BlockSpec((None, None, nq * N_HEADS, tq), lambda bi, qi, *_: (bi, qi, 0, 0)),
                  pl.BlockSpec(memory_space=pl.ANY)],
        out_specs=(pl.BlockSpec((None, N_HEADS, tq, HEAD_DIM), lambda bi, qi, *_: (bi, 0, qi, 0)),
                   pl.BlockSpec((blocks_per_step, N_HEADS, HEAD_DIM), lambda bi, qi, *_: (bi * nq + qi, 0, 0))),
        scratch_shapes=[pltpu.VMEM((ATTN_RING, PAGE_SIZE, N_HEADS, HEAD_DIM), F32),
                        pltpu.SemaphoreType.DMA((2,))])
    return pl.pallas_call(
        functools.partial(_attn_kernel, page_base),
        grid_spec=grid_spec,
        out_shape=(jax.ShapeDtypeStruct((b, N_HEADS, t, HEAD_DIM), BF16),
                   jax.ShapeDtypeStruct((b * nq * blocks_per_step, N_HEADS, HEAD_DIM), F32)),
        compiler_params=pltpu.CompilerParams(dimension_semantics=("arbitrary", "arbitrary"),
                                             vmem_limit_bytes=VMEM_LIMIT),
        name="attn_prompt",
    )(page_table, qt, kb, vt, selt, cache_k)


def _post_body(rnn_ref, o_ref, sga_ref, sgb_ref, x_ref, wbr_ref, wba_ref, wout_ref, wup_ref, wdn_ref,
               g1_ref, b1_ref, bup_ref, bdn_ref, g2_ref, b2_ref, y_ref, stream):
    stream.begin(0)
    y_rnn = jnp.dot(rnn_ref[...], wbr_ref[...], preferred_element_type=F32)
    stream.sums(0, 0, 2)
    o = jnp.concatenate([o_ref[h] for h in range(N_HEADS)], axis=-1)
    y_att = jnp.dot(o, wba_ref[...], preferred_element_type=F32)
    stream.sums(0, 1, 2)
    mix = sga_ref[...] * y_rnn + sgb_ref[...] * y_att
    stream.end(0)
    stream.begin(1)
    t1 = ALPHA * x_ref[...] + jnp.dot(mix.astype(BF16), wout_ref[...], preferred_element_type=F32)
    stream.sums(1, 0, 2)
    x1 = _layer_norm(t1, g1_ref[...], b1_ref[...])
    x1b = x1.astype(BF16)
    acc = ALPHA * x1 + bdn_ref[...]
    stream.sums(1, 1, 2)
    stream.end(1)
    n_chunks = D_FF // D_MODEL
    for c in range(n_chunks):
        stage, part = 2 + c // 2, c % 2
        if part == 0:
            stream.begin(stage)
        cs = slice(c * D_MODEL, (c + 1) * D_MODEL)
        hid = jnp.maximum(jnp.dot(x1b, wup_ref[:, cs], preferred_element_type=F32) + bup_ref[:, cs], 0.0)
        stream.sums(stage, 2 * part, 4)
        acc = acc + jnp.dot((hid * hid).astype(BF16), wdn_ref[cs, :], preferred_element_type=F32)
        stream.sums(stage, 2 * part + 1, 4)
        if part == 1 and stage == 2:
            stream.end(stage)
    y_ref[...] = _layer_norm(acc, g2_ref[...], b2_ref[...])
    stream.end(3)


def _post_kernel(*refs):
    _post_body(*refs, stream=_NoStream())


def _post_stream_kernel(page_base, pt_ref, rnn_ref, o_ref, sga_ref, sgb_ref, x_ref, ck_ref, *rest):
    consts, (y_ref, ksum_ref, ring, sems) = rest[:11], rest[11:]
    n_sub = x_ref.shape[0] // POST_SUB_ROWS
    stream = _PageStream(pt_ref, ck_ref, ring, sems, ksum_ref, pl.program_id(0), pl.num_programs(0),
                         page_base, n_sub * STREAM_CHUNKS)
    stream.prime()
    for sub in range(n_sub):
        rows = pl.ds(sub * POST_SUB_ROWS, POST_SUB_ROWS)
        _post_body(rnn_ref.at[rows], o_ref.at[:, rows], sga_ref.at[rows], sgb_ref.at[rows], x_ref.at[rows],
                   *consts, y_ref.at[rows], stream=_ChunkOffset(stream, sub * STREAM_CHUNKS))
    stream.drain()


def _post(rnn, o, sga, sgb, x, wbr, wba, wout, wup, wdn, g1, b1, bup, bdn, g2, b2, tm, stream=None):
    m = x.shape[0]
    n_steps = m // tm
    tiles_per_seq = o.shape[2] // tm
    assert o.shape[0] * o.shape[2] == m
    tile = pl.BlockSpec((tm, D_MODEL), lambda r, *_: (r, 0))
    o_tile = pl.BlockSpec((None, N_HEADS, tm, HEAD_DIM),
                          lambda r, *_: (r // tiles_per_seq, 0, r % tiles_per_seq, 0))
    consts = (wbr, wba, wout, wup, wdn, g1, b1, bup, bdn, g2, b2)
    params = pltpu.CompilerParams(dimension_semantics=("arbitrary",), vmem_limit_bytes=VMEM_LIMIT)
    y_shape = jax.ShapeDtypeStruct((m, D_MODEL), F32)
    if stream is None:
        return pl.pallas_call(
            _post_kernel,
            grid=(n_steps,),
            in_specs=[tile, o_tile, tile, tile, tile] + [_const_spec(c.shape) for c in consts],
            out_specs=tile,
            out_shape=y_shape,
            compiler_params=params,
            name="post_m%d" % m,
        )(rnn, o, sga, sgb, x, *consts)
    page_table, cache_k, page_base = stream
    assert tm % POST_SUB_ROWS == 0
    blocks_per_step = (tm // POST_SUB_ROWS) * STREAM_CHUNKS * (POST_RING // 2) // PAGES_PER_BLOCK
    n_blocks = blocks_per_step * n_steps
    assert page_base + n_blocks * PAGES_PER_BLOCK == page_table.size, "front + post shares must cover all pages"
    grid_spec = pltpu.PrefetchScalarGridSpec(
        num_scalar_prefetch=1,
        grid=(n_steps,),
        in_specs=([tile, o_tile, tile, tile, tile, pl.BlockSpec(memory_space=pl.ANY)]
                  + [_const_spec(c.shape) for c in consts]),
        out_specs=(tile, pl.BlockSpec((blocks_per_step, N_HEADS, HEAD_DIM), lambda r, *_: (r, 0, 0))),
        scratch_shapes=[pltpu.VMEM((POST_RING, PAGE_SIZE, N_HEADS, HEAD_DIM), F32),
                        pltpu.SemaphoreType.DMA((2,))])
    return pl.pallas_call(
        functools.partial(_post_stream_kernel, page_base),
        grid_spec=grid_spec,
        out_shape=(y_shape, jax.ShapeDtypeStruct((n_blocks, N_HEADS, HEAD_DIM), F32)),
        compiler_params=params,
        name="post_stream_m%d" % m,
    )(page_table, rnn, o, sga, sgb, x, cache_k, *consts)


def _front_sample_kernel(x_ref, p0_ref, p1_ref, p2_ref, h0_ref, w_ref, cw_ref, cb_ref, wg_ref, ba_ref,
                         bx_ref, lam_ref,
                         rnn_ref, q_ref, k_ref, v_ref, sga_ref, sgb_ref, h_ref, u_ref):
    xb = x_ref[...].astype(BF16)

    def proj(j):
        return jnp.dot(xb, w_ref[:, j * D_MODEL:(j + 1) * D_MODEL], preferred_element_type=F32)

    u = proj(0)
    u_ref[...] = u
    uc = (cb_ref[...] + cw_ref[0:1, :] * p0_ref[...] + cw_ref[1:2, :] * p1_ref[...]
          + cw_ref[2:3, :] * p2_ref[...] + cw_ref[3:4, :] * u)
    neg_c_sp = -_softplus(-lam_ref[...])
    ucb = uc.astype(BF16)
    g = proj(1)
    for n in range(RNN_BLOCKS):
        sl = slice(n * RNN_BW, (n + 1) * RNN_BW)
        a, uin = _rglru_block(uc[:, sl], ucb[:, sl], wg_ref[n], ba_ref[:, sl], bx_ref[:, sl],
                              neg_c_sp[:, sl])
        h = a * h0_ref[:, sl] + uin
        h_ref[:, sl] = h
        rnn_ref[:, sl] = (h * _gelu_tanh(g[:, sl])).astype(BF16)
    q_ref[...] = proj(2)
    k_ref[...] = proj(3)
    v_ref[...] = proj(4)
    sga_ref[...] = _sigmoid(proj(5))
    sgb_ref[...] = _sigmoid(proj(6))


def _front_sample(x, p0, p1, p2, h0, w_in, conv_w, conv_b, w_gate, b_a, b_x, lam):
    m = x.shape[0]
    args = (x, p0, p1, p2, h0, w_in, conv_w, conv_b, w_gate, b_a, b_x, lam)
    row = lambda dt: jax.ShapeDtypeStruct((m, D_MODEL), dt)
    return pl.pallas_call(
        _front_sample_kernel,
        grid=(1,),
        in_specs=[_const_spec(a.shape) for a in args],
        out_specs=tuple(pl.BlockSpec((m, D_MODEL), lambda r: (0, 0)) for _ in range(8)),
        out_shape=(row(BF16), row(F32), row(F32), row(F32), row(F32), row(F32), row(F32), row(F32)),
        compiler_params=pltpu.CompilerParams(dimension_semantics=("arbitrary",),
                                             vmem_limit_bytes=VMEM_LIMIT),
        name="front_sample",
    )(*args)


def _sample_attn_kernel(pt_ref, ksum_ref, q_ref, kn_ref, vn_ref, ck_ref, cv_ref, o_ref,
                        kbuf, vbuf, ksem, vsem):
    b = pl.program_id(0)
    n_seq = pl.num_programs(0)
    n_blocks = ksum_ref.shape[0]
    n_gather = MOBA_TOPK * PAGES_PER_BLOCK
    par = lax.rem(b, 2)

    gate = jnp.sum(ksum_ref[...] * (1.0 / MOBA_BLOCK) * q_ref[b][None], axis=-1, keepdims=True)
    blk = lax.broadcasted_iota(jnp.int32, gate.shape, 0)

    def gather_copy(src_ref, dst_ref, sem, page, h, slot, buf):
        return pltpu.make_async_copy(src_ref.at[0, page, :, h, :], dst_ref.at[buf, h, slot], sem.at[buf])

    for t in range(MOBA_TOPK):
        mx = jnp.max(gate, axis=0, keepdims=True)
        idx = jnp.min(jnp.where(gate == mx, blk, n_blocks), axis=0, keepdims=True)
        gate = jnp.where(blk == idx, -jnp.inf, gate)
        for h in range(N_HEADS):
            sel = idx[0, h, 0]
            for pg in range(PAGES_PER_BLOCK):
                page = pt_ref[b, sel * PAGES_PER_BLOCK + pg]
                slot = t * PAGES_PER_BLOCK + pg
                gather_copy(ck_ref, kbuf, ksem, page, h, slot, par).start()
                gather_copy(cv_ref, vbuf, vsem, page, h, slot, par).start()

    def attend(seq, buf):
        for h in range(N_HEADS):
            for slot in range(n_gather):
                gather_copy(ck_ref, kbuf, ksem, 0, h, slot, buf).wait()
                gather_copy(cv_ref, vbuf, vsem, 0, h, slot, buf).wait()
        q = q_ref[seq]
        kn = kn_ref[seq]
        vn = vn_ref[seq]
        for h in range(N_HEADS):
            qh = q[h:h + 1, :]
            kh = kbuf[buf, h].reshape(n_gather * PAGE_SIZE, HEAD_DIM)
            vh = vbuf[buf, h].reshape(n_gather * PAGE_SIZE, HEAD_DIM)
            s = jnp.sum(kh * qh, axis=-1, keepdims=True) * SCALE
            s_new = jnp.sum(kn[h:h + 1, :] * qh, axis=-1, keepdims=True) * SCALE
            m = jnp.maximum(jnp.max(s, axis=0, keepdims=True), s_new)
            p = jnp.exp(s - m)
            p_new = jnp.exp(s_new - m)
            l = jnp.sum(p, axis=0, keepdims=True) + p_new
            o = jnp.sum(p * vh, axis=0, keepdims=True) + p_new * vn[h:h + 1, :]
            o_ref[seq, h:h + 1, :] = o / l

    @pl.when(b > 0)
    def _():
        attend(b - 1, 1 - par)

    @pl.when(b == n_seq - 1)
    def _():
        attend(b, par)


def _sample_attn(page_table, ksum, q, k_new, v_new, cache_k, cache_v):
    n_seq, n_pages = page_table.shape
    assert (n_pages * PAGE_SIZE) % MOBA_BLOCK == 0, "own-block cached prefix is not supported"
    n_blocks = n_pages // PAGES_PER_BLOCK
    assert n_blocks >= MOBA_TOPK and ksum.shape[0] == n_seq * n_blocks
    n_gather = MOBA_TOPK * PAGES_PER_BLOCK
    whole = pl.BlockSpec((n_seq, N_HEADS, HEAD_DIM), lambda s, pt: (0, 0, 0))
    anyspec = pl.BlockSpec(memory_space=pl.ANY)
    gathered = pltpu.VMEM((2, N_HEADS, n_gather, PAGE_SIZE, HEAD_DIM), F32)
    grid_spec = pltpu.PrefetchScalarGridSpec(
        num_scalar_prefetch=1,
        grid=(n_seq,),
        in_specs=[pl.BlockSpec((n_blocks, N_HEADS, HEAD_DIM), lambda s, pt: (s, 0, 0)),
                  whole, whole, whole, anyspec, anyspec],
        out_specs=whole,
        scratch_shapes=[gathered, gathered,
                        pltpu.SemaphoreType.DMA((2,)),
                        pltpu.SemaphoreType.DMA((2,))])
    return pl.pallas_call(
        _sample_attn_kernel,
        grid_spec=grid_spec,
        out_shape=jax.ShapeDtypeStruct((n_seq, N_HEADS, HEAD_DIM), F32),
        compiler_params=pltpu.CompilerParams(dimension_semantics=("arbitrary",),
                                             vmem_limit_bytes=VMEM_LIMIT),
        name="attn_sample",
    )(page_table, ksum, q, k_new, v_new, cache_k, cache_v)


def kernel(x_prompt, x_sample, cache_k, cache_v, state_h, state_conv, page_table, w_in, conv_w, conv_b,
           w_rg_a, b_rg_a, w_rg_x, b_rg_x, lru_lambda, w_br_rnn, w_br_attn, w_out, ln1_g, ln1_b,
           w_up, b_up, w_down, b_down, ln2_g, ln2_b):
    assert w_in.shape[0] == 1, "single-layer trunk"
    b, t, _ = x_prompt.shape
    db = x_sample.shape[0]
    assert x_sample.shape[1] == 1 and t % MOBA_BLOCK == 0

    w_in_b = w_in[0].astype(BF16)
    w_gate = jnp.concatenate([w_rg_a[0], w_rg_x[0]], axis=-1).astype(BF16)
    cw, cb = conv_w[0], conv_b
    post_w = (w_br_rnn[0].astype(BF16), w_br_attn[0].astype(BF16), w_out[0].astype(BF16),
              w_up[0].astype(BF16), w_down[0].astype(BF16), ln1_g, ln1_b, b_up, b_down, ln2_g, ln2_b)

    (rnn_p, qt_p, k_p, v_p, kb_p, vt_p, sga_p, sgb_p, selt_p, h_p, cs_p, ksum_a) = _front_prompt(
        x_prompt, w_in_b, cw, cb, w_gate, b_rg_a, b_rg_x, lru_lambda, page_table, cache_k)
    pages_a = ksum_a.shape[0] * PAGES_PER_BLOCK
    o_p, ksum_b = _attn_prompt(qt_p, kb_p, vt_p, selt_p, page_table, cache_k, pages_a)
    pages_ab = pages_a + ksum_b.shape[0] * PAGES_PER_BLOCK
    flat = lambda a: a.reshape(b * t, D_MODEL)
    y_p, ksum_c = _post(flat(rnn_p), o_p, flat(sga_p), flat(sgb_p), flat(x_prompt), *post_w, tm=2 * POST_SUB_ROWS,
                        stream=(page_table, cache_k, pages_ab))
    ksum = jnp.concatenate([ksum_a, ksum_b, ksum_c], axis=0)

    xs = x_sample.reshape(db, D_MODEL)
    sc = state_conv[0]
    (rnn_s, q_s, k_s, v_s, sga_s, sgb_s, h_s, u_s) = _front_sample(
        xs, sc[:, 0], sc[:, 1], sc[:, 2], state_h[0], w_in_b, cw, cb, w_gate, b_rg_a, b_rg_x, lru_lambda)
    heads = lambda a: a.reshape(db, N_HEADS, HEAD_DIM)
    o_s = _sample_attn(page_table, ksum, heads(q_s), heads(k_s), heads(v_s), cache_k, cache_v)
    o_s_hm = jnp.transpose(o_s, (1, 0, 2)).astype(BF16)[None]
    y_s = _post(rnn_s, o_s_hm, sga_s, sgb_s, xs, *post_w, tm=db)
    cs_s = jnp.concatenate([sc[:, 1:], u_s[:, None, :]], axis=1)

    kv_p = lambda a: a.reshape(1, b, t, N_HEADS, HEAD_DIM)
    kv_s = lambda a: a.reshape(1, db, 1, N_HEADS, HEAD_DIM)
    return (y_p.reshape(b, t, D_MODEL), y_s.reshape(db, 1, D_MODEL), kv_p(k_p), kv_p(v_p),
            h_p.reshape(1, b, D_MODEL), cs_p[None],
            kv_s(k_s), kv_s(v_s), h_s[None], cs_s[None])
```
